```python
import jax, jax.numpy as jnp
from jax import lax
import numpy as np


D_MODEL = 1024
BATCH = 16
SEQ = 4096
DEPTH = 4

GRID_W = 64
CTX_LEN = 256

NA_HEADS = 8
NA_HEAD_DIM = 64
NA_DIM = NA_HEADS * NA_HEAD_DIM
NA_WIN_H = 8
NA_WIN_W = 16

HG_HEADS = 4
HG_HEAD_DIM = 128
HG_DIM = HG_HEADS * HG_HEAD_DIM
HG_CHUNK = 64

EVEN_IN = 3 * NA_DIM + 5 * HG_DIM

CONV_WIDTH = 31

D_FF = -(-8 * D_MODEL // (3 * 256)) * 256

N_EVEN = (DEPTH + 1) // 2
N_ODD = DEPTH // 2
ALPHA = (2 * DEPTH) ** 0.25
BETA = (8 * DEPTH) ** -0.25
LN_EPS = 1e-5
RMS_EPS = 1e-6

kernel_name = 'hybrid_natten_hgrn2_conformer_dit'


def layer_norm(x, g, b):
    xf = x.astype(jnp.float32)
    mu = jnp.mean(xf, axis=-1, keepdims=True)
    var = jnp.mean(jnp.square(xf - mu), axis=-1, keepdims=True)
    y = (xf - mu) * lax.rsqrt(var + LN_EPS) * g.astype(jnp.float32) + b.astype(jnp.float32)
    return y.astype(x.dtype)


def modulate(x, shift, scale):
    return x * (1.0 + scale) + shift


def swiglu(h, w1, w3, w2):
    return (jax.nn.silu(h @ w1) * (h @ w3)) @ w2


def split_heads(a, heads):
    return a.reshape(a.shape[0], a.shape[1], heads, -1)


def context_attention(q, k, v):
    s = jnp.einsum('bqhd,bkhd->bhqk', q, k).astype(jnp.float32) * (q.shape[-1] ** -0.5)
    p = jax.nn.softmax(s, axis=-1).astype(v.dtype)
    o = jnp.einsum('bhqk,bkhd->bqhd', p, v)
    return o.reshape(o.shape[0], o.shape[1], -1)


def neighbourhood_attention(q, k, v, k_ctx, v_ctx, rpb):
    bsz, t, heads, hd = q.shape
    rows = t // GRID_W
    kr = min(NA_WIN_H, rows)
    scale = hd ** -0.5
    qg = q.reshape(bsz, rows, GRID_W, heads, hd)
    kg = k.reshape(bsz, rows, GRID_W, heads, hd)
    vg = v.reshape(bsz, rows, GRID_W, heads, hd)
    col = jnp.arange(GRID_W)
    col_start = jnp.clip(col - NA_WIN_W // 2, 0, GRID_W - NA_WIN_W)
    col_in = (col[None, :] >= col_start[:, None]) & (col[None, :] < col_start[:, None] + NA_WIN_W)
    dc_idx = jnp.clip(col[None, :] - col[:, None] + NA_WIN_W - 1, 0, 2 * NA_WIN_W - 2)
    row_start = jnp.clip(jnp.arange(rows) - kr // 2, 0, rows - kr)

    def row_block(r):
        rs = row_start[r]
        q_r = lax.dynamic_index_in_dim(qg, r, axis=1, keepdims=False)
        k_r = lax.dynamic_slice_in_dim(kg, rs, kr, axis=1)
        v_r = lax.dynamic_slice_in_dim(vg, rs, kr, axis=1)
        dr_idx = rs + jnp.arange(kr) - r + NA_WIN_H - 1
        bias = rpb[:, dr_idx[:, None, None], dc_idx[None, :, :]]
        bias = jnp.transpose(bias, (0, 2, 1, 3)).astype(jnp.float32)
        s_loc = jnp.einsum('bqhd,bikhd->bhqik', q_r, k_r).astype(jnp.float32) * scale + bias[None]
        s_loc = jnp.where(col_in[:, None, :], s_loc, -jnp.inf)
        s_ctx = jnp.einsum('bqhd,bchd->bhqc', q_r, k_ctx).astype(jnp.float32) * scale
        s = jnp.concatenate([s_loc.reshape(bsz, heads, GRID_W, kr * GRID_W), s_ctx], axis=-1)
        p = jax.nn.softmax(s, axis=-1).astype(v.dtype)
        p_loc = p[..., :kr * GRID_W].reshape(bsz, heads, GRID_W, kr, GRID_W)
        return (jnp.einsum('bhqik,bikhd->bqhd', p_loc, v_r)
                + jnp.einsum('bhqc,bchd->bqhd', p[..., kr * GRID_W:], v_ctx))

    o = lax.map(row_block, jnp.arange(rows))
    return jnp.transpose(o, (1, 0, 2, 3, 4)).reshape(bsz, t, heads * hd)


def hgrn2_gates(z, lb):
    z = z.astype(jnp.float32)
    key = (1.0 - lb) * jax.nn.sigmoid(-z)
    log_f = jnp.logaddexp(jnp.log(lb), jnp.log1p(-lb) + jax.nn.log_sigmoid(z))
    return key, log_f


def hgrn2_chunk_scan(q, k, v, log_f, s0):
    bsz, t, heads, dk = q.shape
    dv = v.shape[-1]
    n = t // HG_CHUNK

    def chunks(a):
        return jnp.transpose(a.reshape(bsz, n, HG_CHUNK, heads, a.shape[-1]), (1, 0, 3, 2, 4))

    causal = jnp.tril(jnp.ones((HG_CHUNK, HG_CHUNK), dtype=bool))

    def step(S, xs):
        qc, kc, vc, gc = xs
        b = jnp.cumsum(gc, axis=-2)
        diff = b[..., :, None, :] - b[..., None, :, :]
        decay = jnp.exp(jnp.where(causal[:, :, None], diff, -jnp.inf))
        a = jnp.einsum('bhtd,bhsd,bhtsd->bhts', qc, kc, decay)
        o = jnp.einsum('bhts,bhse->bhte', a, vc) + jnp.einsum('bhtd,bhde->bhte', qc * jnp.exp(b), S)
        b_last = b[..., -1:, :]
        S_new = (jnp.exp(b_last[..., 0, :])[..., None] * S
                 + jnp.einsum('bhsd,bhse->bhde', kc * jnp.exp(b_last - b), vc))
        return S_new, o

    s_final, o = lax.scan(step, s0, (chunks(q), chunks(k), chunks(v), chunks(log_f)))
    o = jnp.transpose(o, (1, 0, 3, 2, 4)).reshape(bsz, t, heads, dv)
    return o, s_final


def hgrn2_final_state(k, v, log_f):
    b = jnp.cumsum(log_f, axis=1)
    return jnp.einsum('bthd,bthe->bhde', k * jnp.exp(b[:, -1:] - b), v)


def hgrn2_direction(q, v, z, q_c, v_c, z_c, lb, reverse, need_ctx_out):
    flip = (lambda a: jnp.flip(a, axis=1)) if reverse else (lambda a: a)
    k_c, g_c = hgrn2_gates(z_c, lb)
    k, g = hgrn2_gates(z, lb)
    if need_ctx_out:
        s0 = jnp.zeros((q.shape[0], HG_HEADS, HG_HEAD_DIM, HG_HEAD_DIM), jnp.float32)
        o_c, s_c = hgrn2_chunk_scan(flip(q_c), flip(k_c), flip(v_c), flip(g_c), s0)
        o_c = flip(o_c)
    else:
        o_c = None
        s_c = hgrn2_final_state(flip(k_c), flip(v_c), flip(g_c))
    o, _ = hgrn2_chunk_scan(flip(q), flip(k), flip(v), flip(g), s_c)
    return flip(o), o_c


def hgrn2_readout(o, g, norm_g):
    o = o * lax.rsqrt(jnp.mean(jnp.square(o), axis=-1, keepdims=True) + RMS_EPS) * norm_g.astype(jnp.float32)
    o = o.reshape(g.shape) * jax.nn.silu(g.astype(jnp.float32))
    return o.astype(g.dtype)


def even_mixer(h, h_c, w_in, w_out, rpb, lb, norm_g, need_ctx_out):
    cuts = [NA_DIM, 2 * NA_DIM, 3 * NA_DIM, 3 * NA_DIM + HG_DIM, 3 * NA_DIM + 2 * HG_DIM,
            3 * NA_DIM + 3 * HG_DIM, 3 * NA_DIM + 4 * HG_DIM]
    qa, ka, va, qb, ib, zf, zb, gb = jnp.split(h @ w_in, cuts, axis=-1)
    qa_c, ka_c, va_c, qb_c, ib_c, zf_c, zb_c, gb_c = jnp.split(h_c @ w_in, cuts, axis=-1)
    na = lambda a: split_heads(a, NA_HEADS)
    hg = lambda a: split_heads(a, HG_HEADS).astype(jnp.float32)
    o_na = neighbourhood_attention(na(qa), na(ka), na(va), na(ka_c), na(va_c), rpb)
    o_f, oc_f = hgrn2_direction(hg(qb), hg(ib), hg(zf), hg(qb_c), hg(ib_c), hg(zf_c), lb[0], False, need_ctx_out)
    o_b, oc_b = hgrn2_direction(hg(qb), hg(ib), hg(zb), hg(qb_c), hg(ib_c), hg(zb_c), lb[1], True, need_ctx_out)
    o_hg = hgrn2_readout(o_f + o_b, gb, norm_g)
    y = jnp.concatenate([o_na, o_hg], axis=-1) @ w_out
    if not need_ctx_out:
        return y, None
    o_na_c = context_attention(na(qa_c), na(ka_c), na(va_c))
    o_hg_c = hgrn2_readout(oc_f + oc_b, gb_c, norm_g)
    y_c = jnp.concatenate([o_na_c, o_hg_c], axis=-1) @ w_out
    return y, y_c


def conv_module(h, w1, b1, w_dw, b_dw, ln_g, ln_b, w2, b2):
    u = h @ w1 + b1
    u = u[..., :D_MODEL] * jax.nn.sigmoid(u[..., D_MODEL:])
    u = lax.conv_general_dilated(u, w_dw[:, None, :], window_strides=(1,),
                                 padding=[(CONV_WIDTH // 2, CONV_WIDTH // 2)],
                                 dimension_numbers=('NWC', 'WIO', 'NWC'),
                                 feature_group_count=D_MODEL) + b_dw
    u = jax.nn.silu(layer_norm(u, ln_g, ln_b))
    return u @ w2 + b2


def setup_inputs(seed: int = 0) -> dict:
    key = jax.random.key(seed)
    ks = jax.random.split(key, 32)
    D = D_MODEL

    def nrm(k, shape, s):
        return jax.random.normal(k, shape, jnp.float32) * s

    return {
        'x': nrm(ks[0], (BATCH, SEQ, D), 1.0),
        'c': nrm(ks[1], (BATCH, D), 1.0),
        'ctx': nrm(ks[2], (BATCH, CTX_LEN, D), 1.0),
        'c_ctx': nrm(ks[3], (D,), 1.0),
        'w_mod': nrm(ks[4], (DEPTH, D, 6 * D), 0.5 * D ** -0.5),
        'b_mod': nrm(ks[5], (DEPTH, 6 * D), 0.01),
        'ln1_g': 1.0 + nrm(ks[6], (DEPTH, D), 0.1),
        'ln1_b': nrm(ks[7], (DEPTH, D), 0.01),
        'ln2_g': 1.0 + nrm(ks[8], (DEPTH, D), 0.1),
        'ln2_b': nrm(ks[9], (DEPTH, D), 0.01),
        'w_in_even': nrm(ks[10], (N_EVEN, D, EVEN_IN), D ** -0.5),
        'w_out_even': nrm(ks[11], (N_EVEN, NA_DIM + HG_DIM, D), BETA * (NA_DIM + HG_DIM) ** -0.5),
        'na_rpb': nrm(ks[12], (N_EVEN, NA_HEADS, 2 * NA_WIN_H - 1, 2 * NA_WIN_W - 1), 0.3),
        'hg_lb': nrm(ks[13], (2, N_EVEN, HG_HEADS, HG_HEAD_DIM), 1.0),
        'hg_norm_g': 1.0 + nrm(ks[14], (N_EVEN, HG_HEADS, HG_HEAD_DIM), 0.1),
        'cv_w1': nrm(ks[15], (N_ODD, D, 2 * D), D ** -0.5),
        'cv_b1': nrm(ks[16], (N_ODD, 2 * D), 0.01),
        'cv_wdw': nrm(ks[17], (N_ODD, CONV_WIDTH, D), CONV_WIDTH ** -0.5),
        'cv_bdw': nrm(ks[18], (N_ODD, D), 0.01),
        'cv_ln_g': 1.0 + nrm(ks[19], (N_ODD, D), 0.1),
        'cv_ln_b': nrm(ks[20], (N_ODD, D), 0.01),
        'cv_w2': nrm(ks[21], (N_ODD, D, D), BETA * D ** -0.5),
        'cv_b2': nrm(ks[22], (N_ODD, D), 0.01),
        'ffn_w1': nrm(ks[23], (DEPTH, D, D_FF), D ** -0.5),
        'ffn_w3': nrm(ks[24], (DEPTH, D, D_FF), D ** -0.5),
        'ffn_w2': nrm(ks[25], (DEPTH, D_FF, D), BETA * D_FF ** -0.5),
    }


def reference(x, c, ctx, c_ctx, w_mod, b_mod, ln1_g, ln1_b, ln2_g, ln2_b, w_in_even, w_out_even,
              na_rpb, hg_lb, hg_norm_g, cv_w1, cv_b1, cv_wdw, cv_bdw, cv_ln_g, cv_ln_b, cv_w2, cv_b2,
              ffn_w1, ffn_w3, ffn_w2):
    last_even = DEPTH - 1 if (DEPTH - 1) % 2 == 0 else DEPTH - 2
    p_lb = jax.nn.softmax(hg_lb.astype(jnp.float32), axis=1)
    hg_lower = jnp.maximum(jnp.cumsum(p_lb, axis=1) - p_lb[:, :1], 0.0)
    xc = ctx
    for l in range(DEPTH):
        ctx_used = l <= last_even
        ctx_out = l < last_even
        mod = jax.nn.silu(c) @ w_mod[l] + b_mod[l]
        sh1, sc1, g1, sh2, sc2, g2 = jnp.split(mod[:, None, :], 6, axis=-1)
        h = modulate(x, sh1, sc1)
        if ctx_used:
            mod_c = jax.nn.silu(c_ctx) @ w_mod[l] + b_mod[l]
            sh1c, sc1c, g1c, sh2c, sc2c, g2c = jnp.split(mod_c, 6, axis=-1)
            h_c = modulate(xc, sh1c, sc1c)
        j = l // 2
        if l % 2 == 0:
            y, y_c = even_mixer(h, h_c, w_in_even[j], w_out_even[j], na_rpb[j], hg_lower[:, j],
                                hg_norm_g[j], ctx_out)
        else:
            y = conv_module(h, cv_w1[j], cv_b1[j], cv_wdw[j], cv_bdw[j], cv_ln_g[j], cv_ln_b[j], cv_w2[j], cv_b2[j])
            if ctx_out:
                y_c = conv_module(h_c, cv_w1[j], cv_b1[j], cv_wdw[j], cv_bdw[j], cv_ln_g[j], cv_ln_b[j],
                                  cv_w2[j], cv_b2[j])
        x = layer_norm(ALPHA * x + g1 * y, ln1_g[l], ln1_b[l])
        x = layer_norm(ALPHA * x + g2 * swiglu(modulate(x, sh2, sc2), ffn_w1[l], ffn_w3[l], ffn_w2[l]),
                       ln2_g[l], ln2_b[l])
        if ctx_out:
            xc = layer_norm(ALPHA * xc + g1c * y_c, ln1_g[l], ln1_b[l])
            xc = layer_norm(ALPHA * xc + g2c * swiglu(modulate(xc, sh2c, sc2c), ffn_w1[l], ffn_w3[l], ffn_w2[l]),
                            ln2_g[l], ln2_b[l])
    return x
```

```python
import functools

import numpy as np
import jax
import jax.numpy as jnp
from jax import lax
from jax.experimental import pallas as pl
from jax.experimental.pallas import tpu as pltpu

F32 = jnp.float32
BF16 = jnp.bfloat16

D_MODEL = 1024
DEPTH = 4
GRID_W = 64
NA_HEADS = 8
NA_HEAD_DIM = 64
NA_DIM = NA_HEADS * NA_HEAD_DIM
NA_WIN_H = 8
NA_WIN_W = 16
HG_HEADS = 4
HG_HEAD_DIM = 128
HG_DIM = HG_HEADS * HG_HEAD_DIM
HG_CHUNK = 64
HG_LEVELS = 6
CONV_WIDTH = 31
CONV_HALO = 16
D_FF = 2816
ALPHA = (2 * DEPTH) ** 0.25
LN_EPS = 1e-5
RMS_EPS = 1e-6

VMEM_LIMIT_BYTES = 56 * 1024 * 1024
MOD_ROWS = 24


def _params(n_grid, vmem=VMEM_LIMIT_BYTES):
    return pltpu.CompilerParams(dimension_semantics=("arbitrary",) * n_grid, vmem_limit_bytes=vmem)


def _const_spec(shape):
    nd = len(shape)
    return pl.BlockSpec(shape, lambda *_: (0,) * nd, pipeline_mode=pl.Buffered(1))


def _layer_norm(x, g, b):
    mu = jnp.mean(x, axis=-1, keepdims=True)
    xc = x - mu
    var = jnp.mean(xc * xc, axis=-1, keepdims=True)
    return xc * lax.rsqrt(var + LN_EPS) * g + b


def _silu(x):
    return x * jax.nn.sigmoid(x)


def _dot(a, b):
    return jnp.dot(a, b, preferred_element_type=F32)


def _dot_nt(a, b):
    return lax.dot_general(a, b, (((1,), (1,)), ((), ())), preferred_element_type=F32)


def _dot_tn(a, b):
    return lax.dot_general(a, b, (((0,), (0,)), ((), ())), preferred_element_type=F32)


def _mod_kernel(c_ref, w_ref, b_ref, o_ref):
    s = _silu(c_ref[...]).astype(BF16)
    o_ref[...] = _dot(s, w_ref[...].astype(BF16)) + b_ref[...]


def _modulation(c_all, w_mod, b_mod):
    depth, d, n = w_mod.shape
    tn = 1536
    return pl.pallas_call(
        _mod_kernel,
        grid=(depth, n // tn),
        in_specs=[
            pl.BlockSpec((MOD_ROWS, d), lambda l, j: (0, 0)),
            pl.BlockSpec((None, d, tn), lambda l, j: (l, 0, j)),
            pl.BlockSpec((None, 1, tn), lambda l, j: (l, 0, j)),
        ],
        out_specs=pl.BlockSpec((None, MOD_ROWS, tn), lambda l, j: (l, 0, j)),
        out_shape=jax.ShapeDtypeStruct((depth, MOD_ROWS, n), F32),
        compiler_params=_params(2),
        name="modulation",
    )(c_all, w_mod, b_mod.reshape(depth, 1, n))


def _mod_spec(mod):
    per_batch = 1 if mod.shape[0] > 1 else 0
    return pl.BlockSpec((None, 1, D_MODEL), lambda b, i: (b * per_batch, 0, 0))


def _inproj_even_kernel(x_ref, sh_ref, sc_ref, w_ref, qa_ref, ka_ref, va_ref, qb_ref, ib_ref, zf_ref,
                        zb_ref, gb_ref):
    h = (x_ref[...] * (1.0 + sc_ref[...]) + sh_ref[...]).astype(BF16)
    outs = (qa_ref, ka_ref, va_ref, qb_ref, ib_ref, zf_ref, zb_ref, gb_ref)
    for s, o_ref in enumerate(outs):
        y = _dot(h, w_ref[:, s * NA_DIM:(s + 1) * NA_DIM])
        if s == 0:
            y = y * (NA_HEAD_DIM ** -0.5)
        o_ref[...] = y.astype(o_ref.dtype)


def _inproj_even(x, sh, sc, w_bf16, tm):
    bsz, t, d = x.shape
    tok = pl.BlockSpec((None, tm, d), lambda b, i: (b, i, 0))
    out_spec = pl.BlockSpec((None, tm, NA_DIM), lambda b, i: (b, i, 0))
    dtypes = (BF16, BF16, BF16, F32, BF16, F32, F32, F32)
    return pl.pallas_call(
        _inproj_even_kernel,
        grid=(bsz, t // tm),
        in_specs=[tok, _mod_spec(sh), _mod_spec(sc), _const_spec(w_bf16.shape)],
        out_specs=[out_spec] * 8,
        out_shape=[jax.ShapeDtypeStruct((bsz, t, NA_DIM), dt) for dt in dtypes],
        compiler_params=_params(2),
        name="inproj_even",
    )(x, sh, sc, w_bf16)


def _inproj_odd_kernel(x_ref, sh_ref, sc_ref, w_ref, b_ref, u_ref):
    h = (x_ref[...] * (1.0 + sc_ref[...]) + sh_ref[...]).astype(BF16)
    a = _dot(h, w_ref[:, :D_MODEL]) + b_ref[:, :D_MODEL]
    g = _dot(h, w_ref[:, D_MODEL:]) + b_ref[:, D_MODEL:]
    u_ref[...] = a * jax.nn.sigmoid(g)


def _inproj_odd(x, sh, sc, w_bf16, bias, tm):
    bsz, t, d = x.shape
    tok = pl.BlockSpec((None, tm, d), lambda b, i: (b, i, 0))
    return pl.pallas_call(
        _inproj_odd_kernel,
        grid=(bsz, t // tm),
        in_specs=[tok, _mod_spec(sh), _mod_spec(sc), _const_spec(w_bf16.shape), _const_spec(bias.shape)],
        out_specs=tok,
        out_shape=jax.ShapeDtypeStruct((bsz, t, d), F32),
        compiler_params=_params(2),
        name="inproj_odd",
    )(x, sh, sc, w_bf16, bias)


HEADS_PER_GROUP = 4
GROUP_W = HEADS_PER_GROUP * NA_HEAD_DIM


def _softmax_pv(s_parts, v_parts):
    m = s_parts[0].max(axis=-1, keepdims=True)
    for s in s_parts[1:]:
        m = jnp.maximum(m, s.max(axis=-1, keepdims=True))
    ps = [jnp.exp(s - m) for s in s_parts]
    l = ps[0].sum(axis=-1, keepdims=True)
    for p in ps[1:]:
        l = l + p.sum(axis=-1, keepdims=True)
    inv = 1.0 / l
    o = _dot((ps[0] * inv).astype(BF16), v_parts[0])
    for p, v in zip(ps[1:], v_parts[1:]):
        o = o + _dot((p * inv).astype(BF16), v)
    return o


def _na_kernel(q_ref, k_ref, v_ref, kc_ref, vc_ref, bias_ref, o_ref, *, rows, kr):
    r = pl.program_id(1)
    rs = jnp.clip(r - kr // 2, 0, rows - kr)
    start = pl.multiple_of(rs * GRID_W, GRID_W)
    nq = q_ref.shape[0]
    lane = lax.broadcasted_iota(jnp.int32, (nq, GROUP_W), 1)
    for g in range(NA_HEADS // HEADS_PER_GROUP):
        cols = slice(g * GROUP_W, (g + 1) * GROUP_W)
        qg = q_ref[:, cols]
        kg = k_ref[pl.ds(start, kr * GRID_W), cols]
        vg = v_ref[pl.ds(start, kr * GRID_W), cols]
        kcg = kc_ref[:, cols]
        vcg = vc_ref[:, cols]
        acc = jnp.zeros((nq, GROUP_W), F32)
        for hh in range(HEADS_PER_GROUP):
            head_lanes = (lane >= hh * NA_HEAD_DIM) & (lane < (hh + 1) * NA_HEAD_DIM)
            qh = jnp.where(head_lanes, qg, jnp.zeros_like(qg))
            s_loc = _dot_nt(qh, kg) + bias_ref[g * HEADS_PER_GROUP + hh]
            s_ctx = _dot_nt(qh, kcg)
            o = _softmax_pv([s_loc, s_ctx], [vg, vcg])
            acc = jnp.where(head_lanes, o, acc)
        o_ref[:, cols] = acc.astype(o_ref.dtype)


def _na_bias_table(rpb, kr):
    col = np.arange(GRID_W)
    col_start = np.clip(col - NA_WIN_W // 2, 0, GRID_W - NA_WIN_W)
    col_in = (col[None, :] >= col_start[:, None]) & (col[None, :] < col_start[:, None] + NA_WIN_W)
    dc_idx = np.clip(col[None, :] - col[:, None] + NA_WIN_W - 1, 0, 2 * NA_WIN_W - 2)
    per_dr = jnp.where(col_in[None, None], rpb[:, :, dc_idx].astype(F32), -jnp.inf)
    tabs = []
    for a in range(kr):
        rows_a = per_dr[:, a:a + kr]
        tabs.append(jnp.transpose(rows_a, (0, 2, 1, 3)).reshape(NA_HEADS, GRID_W, kr * GRID_W))
    return jnp.stack(tabs)


def _neighbourhood_attention(q, k, v, kc, vc, rpb):
    bsz, t, _ = q.shape
    lc = kc.shape[1]
    rows = t // GRID_W
    kr = min(NA_WIN_H, rows)
    bias = _na_bias_table(rpb, kr)

    def bias_index(b, r):
        rs = jnp.clip(r - kr // 2, 0, rows - kr)
        return (rs - r + NA_WIN_H - 1, 0, 0, 0)

    row_spec = pl.BlockSpec((None, GRID_W, NA_DIM), lambda b, r: (b, r, 0))
    seq_spec = pl.BlockSpec((None, t, NA_DIM), lambda b, r: (b, 0, 0))
    ctx_spec = pl.BlockSpec((None, lc, NA_DIM), lambda b, r: (b, 0, 0))
    return pl.pallas_call(
        functools.partial(_na_kernel, rows=rows, kr=kr),
        grid=(bsz, rows),
        in_specs=[row_spec, seq_spec, seq_spec, ctx_spec, ctx_spec,
                  pl.BlockSpec((None, NA_HEADS, GRID_W, kr * GRID_W), bias_index)],
        out_specs=row_spec,
        out_shape=jax.ShapeDtypeStruct((bsz, t, NA_DIM), BF16),
        compiler_params=_params(2),
        name="neighbourhood_attention",
    )(q, k, v, kc, vc, bias)


def _ctx_attn_kernel(q_ref, k_ref, v_ref, o_ref):
    nq = q_ref.shape[0]
    lane = lax.broadcasted_iota(jnp.int32, (nq, GROUP_W), 1)
    for g in range(NA_HEADS // HEADS_PER_GROUP):
        cols = slice(g * GROUP_W, (g + 1) * GROUP_W)
        qg = q_ref[:, cols]
        kg = k_ref[:, cols]
        vg = v_ref[:, cols]
        acc = jnp.zeros((nq, GROUP_W), F32)
        for hh in range(HEADS_PER_GROUP):
            head_lanes = (lane >= hh * NA_HEAD_DIM) & (lane < (hh + 1) * NA_HEAD_DIM)
            qh = jnp.where(head_lanes, qg, jnp.zeros_like(qg))
            o = _softmax_pv([_dot_nt(qh, kg)], [vg])
            acc = jnp.where(head_lanes, o, acc)
        o_ref[:, cols] = acc.astype(o_ref.dtype)


def _context_attention(q, k, v):
    bsz, lc, _ = q.shape
    spec = pl.BlockSpec((None, lc, NA_DIM), lambda b: (b, 0, 0))
    return pl.pallas_call(
        _ctx_attn_kernel,
        grid=(bsz,),
        in_specs=[spec, spec, spec],
        out_specs=spec,
        out_shape=jax.ShapeDtypeStruct((bsz, lc, NA_DIM), BF16),
        compiler_params=_params(1),
        name="context_attention",
    )(q, k, v)


def _hgrn_tables():
    L = HG_CHUNK
    t = np.arange(L)
    p_rows, masks = [], []
    for lev in range(HG_LEVELS):
        c = L >> lev
        mid = (t // c) * c + c // 2
        P = np.zeros((L, L), np.float32)
        for i in range(L):
            m = mid[i]
            if i >= m:
                P[i, m + 1:i + 1] = 1.0
            else:
                P[i, i + 1:m + 1] = 1.0
        p_rows.append(P)
        same = (t[:, None] // c) == (t[None, :] // c)
        masks.append((same & (t[:, None] >= mid[:, None]) & (t[None, :] < mid[None, :])).astype(np.float32))
    incl = np.tril(np.ones((L, L), np.float32))
    after = np.triu(np.ones((L, L), np.float32), 1)
    pf = np.concatenate(p_rows + [incl, after], axis=0)
    mf = np.stack(masks)
    flip = lambda a: a[..., ::-1, ::-1]
    pb = np.concatenate([flip(p) for p in p_rows] + [flip(incl), flip(after)], axis=0)
    mb = np.stack([flip(m) for m in masks])
    return pf, pb, mf, mb


def _hgrn_gates(z, log_lb, log1m_lb, one_m_lb):
    e = jnp.exp(-jnp.abs(z))
    inv = 1.0 / (1.0 + e)
    key = one_m_lb * jnp.where(z >= 0, e * inv, inv)
    c = log1m_lb + (jnp.minimum(z, 0.0) - jnp.log1p(e))
    log_f = jnp.maximum(log_lb, c) + jnp.log1p(jnp.exp(-jnp.abs(log_lb - c)))
    return key, log_f


def _hgrn_chunk(q, v, z, lbp, p_ref, m_ref, st_ref, last_row):
    L = HG_CHUNK
    key, log_f = _hgrn_gates(z, lbp[0], lbp[1], lbp[2])
    g_hi = log_f.astype(BF16)
    g_lo = (log_f - g_hi.astype(F32)).astype(BF16)
    p = p_ref[...]
    decay = jnp.exp(_dot(p, g_hi) + _dot(p, g_lo))
    a = jnp.zeros((L, L), F32)
    for lev in range(HG_LEVELS):
        e = decay[lev * L:(lev + 1) * L]
        a = a + m_ref[lev] * _dot_nt((q * e).astype(BF16), (key * e).astype(BF16))
    row = lax.broadcasted_iota(jnp.int32, (L, L), 0)
    col = lax.broadcasted_iota(jnp.int32, (L, L), 1)
    a = a + jnp.where(row == col, jnp.sum(q * key, axis=-1, keepdims=True), 0.0)
    e_in = decay[HG_LEVELS * L:(HG_LEVELS + 1) * L]
    e_out = decay[(HG_LEVELS + 1) * L:(HG_LEVELS + 2) * L]
    st = st_ref[...]
    o = _dot(a.astype(BF16), v) + _dot_nt((q * e_in).astype(BF16), st.astype(BF16))
    e_all = e_in[last_row:last_row + 1]
    st_ref[...] = st * e_all + _dot_tn(v, (key * e_out).astype(BF16))
    return o


def _hgrn_kernel(q_ref, v_ref, zf_ref, zb_ref, g_ref, qc_ref, vc_ref, zfc_ref, zbc_ref, gc_ref,
                 lb_ref, ng_ref, pf_ref, pb_ref, mf_ref, mb_ref, *rest, ctx_out):
    if ctx_out:
        o_ref, oc_ref, st_ref, of_ref, ofc_ref = rest
    else:
        o_ref, st_ref, of_ref = rest
        oc_ref = ofc_ref = None
    L = HG_CHUNK
    n = q_ref.shape[0] // L
    nc = qc_ref.shape[0] // L
    norm_g = ng_ref[...]

    def rows(c):
        return pl.ds(pl.multiple_of(c * L, L), L)

    def readout(o, g):
        o = o * lax.rsqrt(jnp.mean(o * o, axis=-1, keepdims=True) + RMS_EPS) * norm_g
        return (o * _silu(g)).astype(BF16)

    lbf = (lb_ref[0, 0], lb_ref[0, 1], lb_ref[0, 2])
    st_ref[...] = jnp.zeros_like(st_ref)

    def fwd_ctx(c, carry):
        o = _hgrn_chunk(qc_ref[rows(c), :], vc_ref[rows(c), :], zfc_ref[rows(c), :], lbf, pf_ref, mf_ref,
                        st_ref, L - 1)
        if ctx_out:
            ofc_ref[rows(c), :] = o
        return carry

    def fwd_main(c, carry):
        of_ref[rows(c), :] = _hgrn_chunk(q_ref[rows(c), :], v_ref[rows(c), :], zf_ref[rows(c), :], lbf,
                                         pf_ref, mf_ref, st_ref, L - 1)
        return carry

    lax.fori_loop(0, nc, fwd_ctx, 0)
    lax.fori_loop(0, n, fwd_main, 0)

    lbb = (lb_ref[1, 0], lb_ref[1, 1], lb_ref[1, 2])
    st_ref[...] = jnp.zeros_like(st_ref)

    def bwd_ctx(i, carry):
        c = nc - 1 - i
        o = _hgrn_chunk(qc_ref[rows(c), :], vc_ref[rows(c), :], zbc_ref[rows(c), :], lbb, pb_ref, mb_ref,
                        st_ref, 0)
        if ctx_out:
            oc_ref[rows(c), :] = readout(ofc_ref[rows(c), :] + o, gc_ref[rows(c), :])
        return carry

    def bwd_main(i, carry):
        c = n - 1 - i
        o = _hgrn_chunk(q_ref[rows(c), :], v_ref[rows(c), :], zb_ref[rows(c), :], lbb, pb_ref, mb_ref,
                        st_ref, 0)
        o_ref[rows(c), :] = readout(of_ref[rows(c), :] + o, g_ref[rows(c), :])
        return carry

    lax.fori_loop(0, nc, bwd_ctx, 0)
    lax.fori_loop(0, n, bwd_main, 0)


def _hgrn2(main, ctx, lb, norm_g, ctx_out):
    q = main[0]
    bsz, t, _ = q.shape
    lc = ctx[0].shape[1]
    pf, pb, mf, mb = _hgrn_tables()
    lb = lb.astype(F32)
    lb_params = jnp.stack([jnp.log(lb), jnp.log1p(-lb), 1.0 - lb], axis=1)
    lb_params = jnp.transpose(lb_params, (2, 0, 1, 3))[:, :, :, None, :]
    seq_spec = pl.BlockSpec((None, t, HG_HEAD_DIM), lambda b, h: (b, 0, h))
    ctx_spec = pl.BlockSpec((None, lc, HG_HEAD_DIM), lambda b, h: (b, 0, h))
    in_specs = ([seq_spec] * 5 + [ctx_spec] * 5 + [
        pl.BlockSpec((None, 2, 3, 1, HG_HEAD_DIM), lambda b, h: (h, 0, 0, 0, 0)),
        pl.BlockSpec((None, 1, HG_HEAD_DIM), lambda b, h: (h, 0, 0)),
        _const_spec(pf.shape), _const_spec(pb.shape), _const_spec(mf.shape), _const_spec(mb.shape)])
    out_specs = [seq_spec]
    out_shape = [jax.ShapeDtypeStruct((bsz, t, HG_DIM), BF16)]
    scratch = [pltpu.VMEM((HG_HEAD_DIM, HG_HEAD_DIM), F32), pltpu.VMEM((t, HG_HEAD_DIM), F32)]
    if ctx_out:
        out_specs.append(ctx_spec)
        out_shape.append(jax.ShapeDtypeStruct((bsz, lc, HG_DIM), BF16))
        scratch.append(pltpu.VMEM((lc, HG_HEAD_DIM), F32))
    outs = pl.pallas_call(
        functools.partial(_hgrn_kernel, ctx_out=ctx_out),
        grid=(bsz, HG_HEADS),
        in_specs=in_specs,
        out_specs=out_specs,
        out_shape=out_shape,
        scratch_shapes=scratch,
        compiler_params=_params(2),
        name="hgrn2",
    )(*main, *ctx, lb_params, norm_g.astype(F32)[:, None, :],
      jnp.asarray(pf, BF16), jnp.asarray(pb, BF16), jnp.asarray(mf), jnp.asarray(mb))
    return (outs[0], outs[1]) if ctx_out else (outs[0], None)


LANE_BLOCK = 128


def _conv_kernel(prev_ref, cur_ref, next_ref, w_ref, b_ref, g_ref, beta_ref, o_ref, win_ref, acc_ref):
    i = pl.program_id(1)
    tm = cur_ref.shape[0]
    win_ref[0:CONV_HALO, :] = jnp.where(i > 0, prev_ref[...], 0.0)
    win_ref[CONV_HALO:CONV_HALO + tm, :] = cur_ref[...]
    win_ref[CONV_HALO + tm:, :] = jnp.where(i < pl.num_programs(1) - 1, next_ref[...], 0.0)
    first = CONV_HALO - CONV_WIDTH // 2
    for c in range(D_MODEL // LANE_BLOCK):
        cols = slice(c * LANE_BLOCK, (c + 1) * LANE_BLOCK)
        acc = jnp.zeros((tm, LANE_BLOCK), F32)
        for k in range(CONV_WIDTH):
            acc = acc + win_ref[first + k:first + k + tm, cols] * w_ref[k:k + 1, cols]
        acc_ref[:, cols] = acc + b_ref[:, cols]
    y = _layer_norm(acc_ref[...], g_ref[...], beta_ref[...])
    o_ref[...] = _silu(y).astype(o_ref.dtype)


def _conv_module_mid(u, w_dw, b_dw, ln_g, ln_b, tm):
    bsz, t, d = u.shape
    hb = tm // CONV_HALO
    n_hb = t // CONV_HALO
    vec = lambda a: a.astype(F32).reshape(1, d)
    w_pad = jnp.zeros((32, d), F32).at[:CONV_WIDTH].set(w_dw.astype(F32))
    return pl.pallas_call(
        _conv_kernel,
        grid=(bsz, t // tm),
        in_specs=[
            pl.BlockSpec((None, CONV_HALO, d), lambda b, i: (b, jnp.maximum(i * hb - 1, 0), 0)),
            pl.BlockSpec((None, tm, d), lambda b, i: (b, i, 0)),
            pl.BlockSpec((None, CONV_HALO, d), lambda b, i: (b, jnp.minimum((i + 1) * hb, n_hb - 1), 0)),
            _const_spec((32, d)), _const_spec((1, d)), _const_spec((1, d)), _const_spec((1, d)),
        ],
        out_specs=pl.BlockSpec((None, tm, d), lambda b, i: (b, i, 0)),
        out_shape=jax.ShapeDtypeStruct((bsz, t, d), BF16),
        scratch_shapes=[pltpu.VMEM((tm + 2 * CONV_HALO, d), F32), pltpu.VMEM((tm, d), F32)],
        compiler_params=_params(2),
        name="conv_module",
    )(u, u, u, w_pad, vec(b_dw), vec(ln_g), vec(ln_b))


def _ffn_kernel(*refs, n_act):
    x_ref = refs[0]
    act_refs = refs[1:1 + n_act]
    wo_refs = refs[1 + n_act:1 + 2 * n_act]
    (bo_ref, g1_ref, sh2_ref, sc2_ref, g2_ref, l1g_ref, l1b_ref, l2g_ref, l2b_ref,
     w1_ref, w3_ref, w2_ref, o_ref) = refs[1 + 2 * n_act:]
    y = bo_ref[...]
    for a_ref, w_ref in zip(act_refs, wo_refs):
        y = y + _dot(a_ref[...], w_ref[...])
    x1 = _layer_norm(ALPHA * x_ref[...] + g1_ref[...] * y, l1g_ref[...], l1b_ref[...])
    h = (x1 * (1.0 + sc2_ref[...]) + sh2_ref[...]).astype(BF16)
    u = (_silu(_dot(h, w1_ref[...])) * _dot(h, w3_ref[...])).astype(BF16)
    f = _dot(u, w2_ref[...])
    o_ref[...] = _layer_norm(ALPHA * x1 + g2_ref[...] * f, l2g_ref[...], l2b_ref[...])


def _out_ffn(x, acts, w_outs, b_out, mods, ln, w1, w3, w2, tm):
    bsz, t, d = x.shape
    n_act = len(acts)
    tok = lambda width: pl.BlockSpec((None, tm, width), lambda b, i: (b, i, 0))
    vec = lambda a: a.astype(F32).reshape(1, d)
    g1, sh2, sc2, g2 = mods
    in_specs = ([tok(d)] + [tok(a.shape[-1]) for a in acts] + [_const_spec(w.shape) for w in w_outs]
                + [_const_spec((1, d))] + [_mod_spec(m) for m in mods] + [_const_spec((1, d))] * 4
                + [_const_spec(w1.shape), _const_spec(w3.shape), _const_spec(w2.shape)])
    return pl.pallas_call(
        functools.partial(_ffn_kernel, n_act=n_act),
        grid=(bsz, t // tm),
        in_specs=in_specs,
        out_specs=tok(d),
        out_shape=jax.ShapeDtypeStruct((bsz, t, d), F32),
        compiler_params=_params(2),
        name="out_ffn",
    )(x, *acts, *w_outs, vec(b_out), g1, sh2, sc2, g2, *[vec(a) for a in ln], w1, w3, w2)


def kernel(x, c, ctx, c_ctx, w_mod, b_mod, ln1_g, ln1_b, ln2_g, ln2_b, w_in_even, w_out_even, na_rpb, hg_lb,
           hg_norm_g, cv_w1, cv_b1, cv_wdw, cv_bdw, cv_ln_g, cv_ln_b, cv_w2, cv_b2, ffn_w1, ffn_w3, ffn_w2):
    bsz, t, d = x.shape
    lc = ctx.shape[1]
    depth = w_mod.shape[0]
    last_even = depth - 1 if (depth - 1) % 2 == 0 else depth - 2
    tm = min(512, t)
    tmc = min(512, lc)

    p_lb = jax.nn.softmax(hg_lb.astype(F32), axis=1)
    hg_lower = jnp.maximum(jnp.cumsum(p_lb, axis=1) - p_lb[:, :1], 0.0)

    c_all = jnp.zeros((MOD_ROWS, d), F32).at[:bsz].set(c).at[bsz].set(c_ctx)
    mod_all = _modulation(c_all, w_mod, b_mod)

    xc = ctx
    for l in range(depth):
        ctx_used = l <= last_even
        ctx_out = l < last_even
        mod = [mod_all[l, :bsz, k * d:(k + 1) * d].reshape(bsz, 1, d) for k in range(6)]
        mod_c = [mod_all[l, bsz:bsz + 1, k * d:(k + 1) * d].reshape(1, 1, d) for k in range(6)]
        ln = (ln1_g[l], ln1_b[l], ln2_g[l], ln2_b[l])
        w1, w3, w2 = ffn_w1[l].astype(BF16), ffn_w3[l].astype(BF16), ffn_w2[l].astype(BF16)
        j = l // 2
        if l % 2 == 0:
            w_in = w_in_even[j].astype(BF16)
            w_out = w_out_even[j].astype(BF16)
            w_outs = [w_out[:NA_DIM], w_out[NA_DIM:]]
            b_out = jnp.zeros((d,), F32)
            qa, ka, va, qb, ib, zf, zb, gb = _inproj_even(x, mod[0], mod[1], w_in, tm)
            qa_c, ka_c, va_c, qb_c, ib_c, zf_c, zb_c, gb_c = _inproj_even(xc, mod_c[0], mod_c[1], w_in, tmc)
            o_na = _neighbourhood_attention(qa, ka, va, ka_c, va_c, na_rpb[j])
            o_hg, o_hg_c = _hgrn2((qb, ib, zf, zb, gb), (qb_c, ib_c, zf_c, zb_c, gb_c), hg_lower[:, j],
                                  hg_norm_g[j], ctx_out)
            acts = [o_na, o_hg]
            if ctx_out:
                acts_c = [_context_attention(qa_c, ka_c, va_c), o_hg_c]
        else:
            w_outs = [cv_w2[j].astype(BF16)]
            b_out = cv_b2[j]
            conv = (cv_wdw[j], cv_bdw[j], cv_ln_g[j], cv_ln_b[j])
            w_cv1 = cv_w1[j].astype(BF16)
            b_cv1 = cv_b1[j].astype(F32).reshape(1, 2 * d)
            acts = [_conv_module_mid(_inproj_odd(x, mod[0], mod[1], w_cv1, b_cv1, tm), *conv, min(256, t))]
            if ctx_out:
                acts_c = [_conv_module_mid(_inproj_odd(xc, mod_c[0], mod_c[1], w_cv1, b_cv1, tmc), *conv,
                                           min(256, lc))]
        x = _out_ffn(x, acts, w_outs, b_out, mod[2:], ln, w1, w3, w2, tm)
        if ctx_out:
            xc = _out_ffn(xc, acts_c, w_outs, b_out, mod_c[2:], ln, w1, w3, w2, tmc)
    return x
```

```python
import functools

import numpy as np
import jax
import jax.numpy as jnp
from jax import lax
from jax.experimental import pallas as pl
from jax.experimental.pallas import tpu as pltpu

F32 = jnp.float32
BF16 = jnp.bfloat16

D_MODEL = 1024
DEPTH = 4
GRID_W = 64
NA_HEADS = 8
NA_HEAD_DIM = 64
NA_DIM = NA_HEADS * NA_HEAD_DIM
NA_WIN_H = 8
NA_WIN_W = 16
HG_HEADS = 4
HG_HEAD_DIM = 128
HG_DIM = HG_HEADS * HG_HEAD_DIM
HG_CHUNK = 64
HG_LEVELS = 6
HG_GROUP = 4
LOG2_E = 1.4426950408889634
CONV_WIDTH = 31
CONV_HALO = 16
D_FF = 2816
ALPHA = (2 * DEPTH) ** 0.25
LN_EPS = 1e-5
RMS_EPS = 1e-6

VMEM_LIMIT_BYTES = 56 * 1024 * 1024
MOD_ROWS = 24


def _params(n_grid, vmem=VMEM_LIMIT_BYTES):
    return pltpu.CompilerParams(dimension_semantics=("arbitrary",) * n_grid, vmem_limit_bytes=vmem)


def _const_spec(shape):
    nd = len(shape)
    return pl.BlockSpec(shape, lambda *_: (0,) * nd, pipeline_mode=pl.Buffered(1))


def _layer_norm(x, g, b):
    mu = jnp.mean(x, axis=-1, keepdims=True)
    xc = x - mu
    var = jnp.mean(xc * xc, axis=-1, keepdims=True)
    return xc * lax.rsqrt(var + LN_EPS) * g + b


def _silu(x):
    return x * jax.nn.sigmoid(x)


def _dot(a, b):
    return jnp.dot(a, b, preferred_element_type=F32)


def _dot_nt(a, b):
    return lax.dot_general(a, b, (((1,), (1,)), ((), ())), preferred_element_type=F32)


def _dot_tn(a, b):
    return lax.dot_general(a, b, (((0,), (0,)), ((), ())), preferred_element_type=F32)


def _mod_kernel(c_ref, w_ref, b_ref, o_ref):
    s = _silu(c_ref[...]).astype(BF16)
    o_ref[...] = _dot(s, w_ref[...].astype(BF16)) + b_ref[...]


def _modulation(c_all, w_mod, b_mod):
    depth, d, n = w_mod.shape
    tn = 1536
    return pl.pallas_call(
        _mod_kernel,
        grid=(depth, n // tn),
        in_specs=[
            pl.BlockSpec((MOD_ROWS, d), lambda l, j: (0, 0)),
            pl.BlockSpec((None, d, tn), lambda l, j: (l, 0, j)),
            pl.BlockSpec((None, 1, tn), lambda l, j: (l, 0, j)),
        ],
        out_specs=pl.BlockSpec((None, MOD_ROWS, tn), lambda l, j: (l, 0, j)),
        out_shape=jax.ShapeDtypeStruct((depth, MOD_ROWS, n), F32),
        compiler_params=_params(2),
        name="modulation",
    )(c_all, w_mod, b_mod.reshape(depth, 1, n))


def _mod_spec(mod):
    per_batch = 1 if mod.shape[0] > 1 else 0
    return pl.BlockSpec((None, 1, D_MODEL), lambda b, i: (b * per_batch, 0, 0))


def _inproj_even_kernel(x_ref, sh_ref, sc_ref, w_ref, qa_ref, ka_ref, va_ref, qb_ref, ib_ref, zf_ref,
                        zb_ref, gb_ref):
    h = (x_ref[...] * (1.0 + sc_ref[...]) + sh_ref[...]).astype(BF16)
    outs = (qa_ref, ka_ref, va_ref, qb_ref, ib_ref, zf_ref, zb_ref, gb_ref)
    for s, o_ref in enumerate(outs):
        y = _dot(h, w_ref[:, s * NA_DIM:(s + 1) * NA_DIM])
        if s == 0:
            y = y * (NA_HEAD_DIM ** -0.5)
        o_ref[...] = y.astype(o_ref.dtype)


def _inproj_even(x, sh, sc, w_bf16, tm):
    bsz, t, d = x.shape
    tok = pl.BlockSpec((None, tm, d), lambda b, i: (b, i, 0))
    out_spec = pl.BlockSpec((None, tm, NA_DIM), lambda b, i: (b, i, 0))
    dtypes = (BF16, BF16, BF16, F32, BF16, F32, F32, F32)
    return pl.pallas_call(
        _inproj_even_kernel,
        grid=(bsz, t // tm),
        in_specs=[tok, _mod_spec(sh), _mod_spec(sc), _const_spec(w_bf16.shape)],
        out_specs=[out_spec] * 8,
        out_shape=[jax.ShapeDtypeStruct((bsz, t, NA_DIM), dt) for dt in dtypes],
        compiler_params=_params(2),
        name="inproj_even",
    )(x, sh, sc, w_bf16)


def _inproj_odd_kernel(x_ref, sh_ref, sc_ref, w_ref, b_ref, u_ref):
    h = (x_ref[...] * (1.0 + sc_ref[...]) + sh_ref[...]).astype(BF16)
    a = _dot(h, w_ref[:, :D_MODEL]) + b_ref[:, :D_MODEL]
    g = _dot(h, w_ref[:, D_MODEL:]) + b_ref[:, D_MODEL:]
    u_ref[...] = a * jax.nn.sigmoid(g)


def _inproj_odd(x, sh, sc, w_bf16, bias, tm):
    bsz, t, d = x.shape
    tok = pl.BlockSpec((None, tm, d), lambda b, i: (b, i, 0))
    return pl.pallas_call(
        _inproj_odd_kernel,
        grid=(bsz, t // tm),
        in_specs=[tok, _mod_spec(sh), _mod_spec(sc), _const_spec(w_bf16.shape), _const_spec(bias.shape)],
        out_specs=tok,
        out_shape=jax.ShapeDtypeStruct((bsz, t, d), F32),
        compiler_params=_params(2),
        name="inproj_odd",
    )(x, sh, sc, w_bf16, bias)


HEADS_PER_GROUP = 4
GROUP_W = HEADS_PER_GROUP * NA_HEAD_DIM
NA_TILE_ROWS = 4


def _softmax_pv(s_parts, v_parts):
    m = s_parts[0].max(axis=-1, keepdims=True)
    for s in s_parts[1:]:
        m = jnp.maximum(m, s.max(axis=-1, keepdims=True))
    ps = [jnp.exp(s - m) for s in s_parts]
    l = ps[0].sum(axis=-1, keepdims=True)
    for p in ps[1:]:
        l = l + p.sum(axis=-1, keepdims=True)
    inv = 1.0 / l
    o = _dot((ps[0] * inv).astype(BF16), v_parts[0])
    for p, v in zip(ps[1:], v_parts[1:]):
        o = o + _dot((p * inv).astype(BF16), v)
    return o


def _na_kernel(q_ref, k_ref, v_ref, kc_ref, vc_ref, bias_ref, o_ref, *, rows, key_rows):
    r0 = pl.program_id(1) * NA_TILE_ROWS
    first_row = jnp.clip(r0 - NA_WIN_H // 2, 0, rows - key_rows)
    start = pl.multiple_of(first_row * GRID_W, GRID_W)
    nq = q_ref.shape[0]
    nk = key_rows * GRID_W
    lane = lax.broadcasted_iota(jnp.int32, (nq, GROUP_W), 1)
    for g in range(NA_HEADS // HEADS_PER_GROUP):
        cols = slice(g * GROUP_W, (g + 1) * GROUP_W)
        qg = q_ref[:, cols]
        kg = k_ref[pl.ds(start, nk), cols]
        vg = v_ref[pl.ds(start, nk), cols]
        kcg = kc_ref[:, cols]
        vcg = vc_ref[:, cols]
        acc = jnp.zeros((nq, GROUP_W), F32)
        for hh in range(HEADS_PER_GROUP):
            head_lanes = (lane >= hh * NA_HEAD_DIM) & (lane < (hh + 1) * NA_HEAD_DIM)
            qh = jnp.where(head_lanes, qg, jnp.zeros_like(qg))
            s_loc = _dot_nt(qh, kg) + bias_ref[g * HEADS_PER_GROUP + hh]
            s_ctx = _dot_nt(qh, kcg)
            o = _softmax_pv([s_loc, s_ctx], [vg, vcg])
            acc = jnp.where(head_lanes, o, acc)
        o_ref[:, cols] = acc.astype(o_ref.dtype)


def _na_tile_geometry(rows):
    kr = min(NA_WIN_H, rows)
    key_rows = min(NA_TILE_ROWS + NA_WIN_H, rows)
    outside = 2 * NA_WIN_H - 1
    geo = []
    for r0 in range(0, rows, NA_TILE_ROWS):
        first = int(np.clip(r0 - NA_WIN_H // 2, 0, rows - key_rows))
        idx = np.full((NA_TILE_ROWS, key_rows), outside, np.int32)
        for i in range(NA_TILE_ROWS):
            r = r0 + i
            rs = int(np.clip(r - kr // 2, 0, rows - kr))
            assert first <= rs and rs + kr <= first + key_rows
            for j in range(rs - first, rs - first + kr):
                idx[i, j] = first + j - r + NA_WIN_H - 1
        geo.append(idx)
    return key_rows, geo


def _na_bias_table(rpb, rows):
    key_rows, geo = _na_tile_geometry(rows)
    assert all(np.array_equal(g, geo[1]) for g in geo[1:-1])
    col = np.arange(GRID_W)
    col_start = np.clip(col - NA_WIN_W // 2, 0, GRID_W - NA_WIN_W)
    col_in = (col[None, :] >= col_start[:, None]) & (col[None, :] < col_start[:, None] + NA_WIN_W)
    dc_idx = np.clip(col[None, :] - col[:, None] + NA_WIN_W - 1, 0, 2 * NA_WIN_W - 2)
    per_dr = jnp.where(col_in[None, None], rpb[:, :, dc_idx].astype(F32), -jnp.inf)
    per_dr = jnp.concatenate([per_dr, jnp.full_like(per_dr[:, :1], -jnp.inf)], axis=1)
    tabs = []
    for idx in (geo[0], geo[1], geo[-1]):
        blocks = per_dr[:, idx]
        tabs.append(jnp.transpose(blocks, (0, 1, 3, 2, 4)).reshape(
            NA_HEADS, NA_TILE_ROWS * GRID_W, key_rows * GRID_W))
    return key_rows, jnp.stack(tabs)


def _neighbourhood_attention(q, k, v, kc, vc, rpb):
    bsz, t, _ = q.shape
    lc = kc.shape[1]
    rows = t // GRID_W
    n_tiles = rows // NA_TILE_ROWS
    assert n_tiles >= 3
    key_rows, bias = _na_bias_table(rpb, rows)
    nq = NA_TILE_ROWS * GRID_W

    def bias_index(b, i):
        return (jnp.minimum(i, 1) + (i == n_tiles - 1).astype(jnp.int32), 0, 0, 0)

    tile_spec = pl.BlockSpec((None, nq, NA_DIM), lambda b, i: (b, i, 0))
    seq_spec = pl.BlockSpec((None, t, NA_DIM), lambda b, i: (b, 0, 0))
    ctx_spec = pl.BlockSpec((None, lc, NA_DIM), lambda b, i: (b, 0, 0))
    return pl.pallas_call(
        functools.partial(_na_kernel, rows=rows, key_rows=key_rows),
        grid=(bsz, n_tiles),
        in_specs=[tile_spec, seq_spec, seq_spec, ctx_spec, ctx_spec,
                  pl.BlockSpec((None, NA_HEADS, nq, key_rows * GRID_W), bias_index)],
        out_specs=tile_spec,
        out_shape=jax.ShapeDtypeStruct((bsz, t, NA_DIM), BF16),
        compiler_params=_params(2),
        name="neighbourhood_attention",
    )(q, k, v, kc, vc, bias)


def _ctx_attn_kernel(q_ref, k_ref, v_ref, o_ref):
    nq = q_ref.shape[0]
    lane = lax.broadcasted_iota(jnp.int32, (nq, GROUP_W), 1)
    for g in range(NA_HEADS // HEADS_PER_GROUP):
        cols = slice(g * GROUP_W, (g + 1) * GROUP_W)
        qg = q_ref[:, cols]
        kg = k_ref[:, cols]
        vg = v_ref[:, cols]
        acc = jnp.zeros((nq, GROUP_W), F32)
        for hh in range(HEADS_PER_GROUP):
            head_lanes = (lane >= hh * NA_HEAD_DIM) & (lane < (hh + 1) * NA_HEAD_DIM)
            qh = jnp.where(head_lanes, qg, jnp.zeros_like(qg))
            o = _softmax_pv([_dot_nt(qh, kg)], [vg])
            acc = jnp.where(head_lanes, o, acc)
        o_ref[:, cols] = acc.astype(o_ref.dtype)


def _context_attention(q, k, v):
    bsz, lc, _ = q.shape
    spec = pl.BlockSpec((None, lc, NA_DIM), lambda b: (b, 0, 0))
    return pl.pallas_call(
        _ctx_attn_kernel,
        grid=(bsz,),
        in_specs=[spec, spec, spec],
        out_specs=spec,
        out_shape=jax.ShapeDtypeStruct((bsz, lc, NA_DIM), BF16),
        compiler_params=_params(1),
        name="context_attention",
    )(q, k, v)


def _hgrn_tables():
    L = HG_CHUNK
    t = np.arange(L)
    p_rows, masks = [], []
    for lev in range(HG_LEVELS):
        c = L >> lev
        mid = (t // c) * c + c // 2
        P = np.zeros((L, L), np.float32)
        for i in range(L):
            m = mid[i]
            if i >= m:
                P[i, m + 1:i + 1] = 1.0
            else:
                P[i, i + 1:m + 1] = 1.0
        p_rows.append(P)
        same = (t[:, None] // c) == (t[None, :] // c)
        masks.append((same & (t[:, None] >= mid[:, None]) & (t[None, :] < mid[None, :])).astype(np.float32))
    incl = np.tril(np.ones((L, L), np.float32))
    after = np.triu(np.ones((L, L), np.float32), 1)
    pf = np.concatenate(p_rows + [incl, after], axis=0)
    mf = np.stack(masks)
    flip = lambda a: a[..., ::-1, ::-1]
    pb = np.concatenate([flip(p) for p in p_rows] + [flip(incl), flip(after)], axis=0)
    mb = np.stack([flip(m) for m in masks])
    return pf, pb, mf, mb


def _hgrn_gates(z, log_lb, log1m_lb, one_m_lb):
    e = jnp.exp(-jnp.abs(z))
    inv = 1.0 / (1.0 + e)
    key = one_m_lb * jnp.where(z >= 0, e * inv, inv)
    c = log1m_lb + (jnp.minimum(z, 0.0) - jnp.log1p(e))
    log_f = jnp.maximum(log_lb, c) + jnp.log1p(jnp.exp(-jnp.abs(log_lb - c)))
    return key, log_f


def _hgrn_group(q_ref, v_ref, z_ref, first_chunk, lbp, p_ref, m_ref, st_ref, reverse, emit):
    L, G, dk = HG_CHUNK, HG_GROUP, HG_HEAD_DIM
    base = pl.multiple_of(first_chunk * L, G * L)
    group_rows = pl.ds(base, G * L)
    chunk = lambda a, j: a[j * L:(j + 1) * L]
    wide = lambda a: jnp.concatenate([chunk(a, j) for j in range(G)], axis=1)

    q, v, z = q_ref[group_rows, :], v_ref[group_rows, :], z_ref[group_rows, :]
    key, log_f = _hgrn_gates(z, lbp[0], lbp[1], lbp[2])
    log2_f = log_f * LOG2_E
    g_hi = log2_f.astype(BF16)
    g_lo = (log2_f - g_hi.astype(F32)).astype(BF16)
    p = p_ref[...]
    decay = jnp.exp2(_dot(p, wide(g_hi)) + _dot(p, wide(g_lo)))
    dec = lambda blk, j: decay[blk * L:(blk + 1) * L, j * dk:(j + 1) * dk]
    qs = [chunk(q, j) for j in range(G)]
    ks = [chunk(key, j) for j in range(G)]
    vs = [chunk(v, j) for j in range(G)]

    row = lax.broadcasted_iota(jnp.int32, (L, L), 0)
    col = lax.broadcasted_iota(jnp.int32, (L, L), 1)
    a = [jnp.where(row == col, jnp.sum(qs[j] * ks[j], axis=-1, keepdims=True), 0.0) for j in range(G)]
    for lev in range(HG_LEVELS):
        for j in range(G):
            e = dec(lev, j)
            a[j] = a[j] + m_ref[lev] * _dot_nt((qs[j] * e).astype(BF16), (ks[j] * e).astype(BF16))
    o_local = [_dot(a[j].astype(BF16), vs[j]) for j in range(G)]
    q_in = [(qs[j] * dec(HG_LEVELS, j)).astype(BF16) for j in range(G)]
    st_add = [_dot_tn(vs[j], (ks[j] * dec(HG_LEVELS + 1, j)).astype(BF16)) for j in range(G)]
    last_row = 0 if reverse else L - 1
    e_all = [dec(HG_LEVELS, j)[last_row:last_row + 1] for j in range(G)]

    st = st_ref[...]
    for j in (reversed(range(G)) if reverse else range(G)):
        emit(pl.ds(base + j * L, L), o_local[j] + _dot_nt(q_in[j], st.astype(BF16)))
        st = st * e_all[j] + st_add[j]
    st_ref[...] = st


def _hgrn_kernel(q_ref, v_ref, zf_ref, zb_ref, g_ref, qc_ref, vc_ref, zfc_ref, zbc_ref, gc_ref,
                 lb_ref, ng_ref, pf_ref, pb_ref, mf_ref, mb_ref, *rest, ctx_out):
    if ctx_out:
        o_ref, oc_ref, st_ref, of_ref, ofc_ref = rest
    else:
        o_ref, st_ref, of_ref = rest
        oc_ref = ofc_ref = None
    group_rows = HG_GROUP * HG_CHUNK
    n = q_ref.shape[0] // group_rows
    nc = qc_ref.shape[0] // group_rows
    norm_g = ng_ref[...]

    def readout(o, g):
        o = o * lax.rsqrt(jnp.mean(o * o, axis=-1, keepdims=True) + RMS_EPS) * norm_g
        return (o * _silu(g)).astype(BF16)

    def store_to(dst_ref):
        def emit(rows, o):
            dst_ref[rows, :] = o
        return emit

    def readout_to(dst_ref, fwd_ref, gate_ref):
        def emit(rows, o):
            dst_ref[rows, :] = readout(fwd_ref[rows, :] + o, gate_ref[rows, :])
        return emit

    def drop(rows, o):
        pass

    def scan(refs, n_groups, lbp, p_ref, m_ref, reverse, emit):
        def body(i, carry):
            first_chunk = ((n_groups - 1 - i) if reverse else i) * HG_GROUP
            _hgrn_group(*refs, first_chunk, lbp, p_ref, m_ref, st_ref, reverse, emit)
            return carry
        lax.fori_loop(0, n_groups, body, 0)

    lbf = (lb_ref[0, 0], lb_ref[0, 1], lb_ref[0, 2])
    st_ref[...] = jnp.zeros_like(st_ref)
    scan((qc_ref, vc_ref, zfc_ref), nc, lbf, pf_ref, mf_ref, False, store_to(ofc_ref) if ctx_out else drop)
    scan((q_ref, v_ref, zf_ref), n, lbf, pf_ref, mf_ref, False, store_to(of_ref))

    lbb = (lb_ref[1, 0], lb_ref[1, 1], lb_ref[1, 2])
    st_ref[...] = jnp.zeros_like(st_ref)
    scan((qc_ref, vc_ref, zbc_ref), nc, lbb, pb_ref, mb_ref, True,
         readout_to(oc_ref, ofc_ref, gc_ref) if ctx_out else drop)
    scan((q_ref, v_ref, zb_ref), n, lbb, pb_ref, mb_ref, True, readout_to(o_ref, of_ref, g_ref))


def _hgrn2(main, ctx, lb, norm_g, ctx_out):
    q = main[0]
    bsz, t, _ = q.shape
    lc = ctx[0].shape[1]
    pf, pb, mf, mb = _hgrn_tables()
    lb = lb.astype(F32)
    lb_params = jnp.stack([jnp.log(lb), jnp.log1p(-lb), 1.0 - lb], axis=1)
    lb_params = jnp.transpose(lb_params, (2, 0, 1, 3))[:, :, :, None, :]
    seq_spec = pl.BlockSpec((None, t, HG_HEAD_DIM), lambda b, h: (b, 0, h))
    ctx_spec = pl.BlockSpec((None, lc, HG_HEAD_DIM), lambda b, h: (b, 0, h))
    in_specs = ([seq_spec] * 5 + [ctx_spec] * 5 + [
        pl.BlockSpec((None, 2, 3, 1, HG_HEAD_DIM), lambda b, h: (h, 0, 0, 0, 0)),
        pl.BlockSpec((None, 1, HG_HEAD_DIM), lambda b, h: (h, 0, 0)),
        _const_spec(pf.shape), _const_spec(pb.shape), _const_spec(mf.shape), _const_spec(mb.shape)])
    out_specs = [seq_spec]
    out_shape = [jax.ShapeDtypeStruct((bsz, t, HG_DIM), BF16)]
    scratch = [pltpu.VMEM((HG_HEAD_DIM, HG_HEAD_DIM), F32), pltpu.VMEM((t, HG_HEAD_DIM), F32)]
    if ctx_out:
        out_specs.append(ctx_spec)
        out_shape.append(jax.ShapeDtypeStruct((bsz, lc, HG_DIM), BF16))
        scratch.append(pltpu.VMEM((lc, HG_HEAD_DIM), F32))
    outs = pl.pallas_call(
        functools.partial(_hgrn_kernel, ctx_out=ctx_out),
        grid=(bsz, HG_HEADS),
        in_specs=in_specs,
        out_specs=out_specs,
        out_shape=out_shape,
        scratch_shapes=scratch,
        compiler_params=_params(2),
        name="hgrn2",
    )(*main, *ctx, lb_params, norm_g.astype(F32)[:, None, :],
      jnp.asarray(pf, BF16), jnp.asarray(pb, BF16), jnp.asarray(mf), jnp.asarray(mb))
    return (outs[0], outs[1]) if ctx_out else (outs[0], None)


LANE_BLOCK = 128
SUBLANES = 8


def _conv_kernel(prev_ref, cur_ref, next_ref, w_ref, b_ref, g_ref, beta_ref, o_ref, win_ref, acc_ref,
                 shift_ref):
    i = pl.program_id(1)
    tm = cur_ref.shape[0]
    win_ref[0:CONV_HALO, :] = jnp.where(i > 0, prev_ref[...], 0.0)
    win_ref[CONV_HALO:CONV_HALO + tm, :] = cur_ref[...]
    win_ref[CONV_HALO + tm:, :] = jnp.where(i < pl.num_programs(1) - 1, next_ref[...], 0.0)
    first = CONV_HALO - CONV_WIDTH // 2
    span = tm + 2 * CONV_HALO - SUBLANES
    for c in range(D_MODEL // LANE_BLOCK):
        cols = slice(c * LANE_BLOCK, (c + 1) * LANE_BLOCK)
        for phase in range(SUBLANES):
            shift_ref[phase] = win_ref[phase:phase + span, cols]
        acc = jnp.zeros((tm, LANE_BLOCK), F32)
        for k in range(CONV_WIDTH):
            phase = (first + k) % SUBLANES
            off = first + k - phase
            acc = acc + shift_ref[phase, off:off + tm, :] * w_ref[k, :, cols]
        acc_ref[:, cols] = acc + b_ref[:, cols]
    y = _layer_norm(acc_ref[...], g_ref[...], beta_ref[...])
    o_ref[...] = _silu(y).astype(o_ref.dtype)


def _conv_module_mid(u, w_dw, b_dw, ln_g, ln_b, tm):
    bsz, t, d = u.shape
    hb = tm // CONV_HALO
    n_hb = t // CONV_HALO
    vec = lambda a: a.astype(F32).reshape(1, d)
    w_taps = w_dw.astype(F32).reshape(CONV_WIDTH, 1, d)
    return pl.pallas_call(
        _conv_kernel,
        grid=(bsz, t // tm),
        in_specs=[
            pl.BlockSpec((None, CONV_HALO, d), lambda b, i: (b, jnp.maximum(i * hb - 1, 0), 0)),
            pl.BlockSpec((None, tm, d), lambda b, i: (b, i, 0)),
            pl.BlockSpec((None, CONV_HALO, d), lambda b, i: (b, jnp.minimum((i + 1) * hb, n_hb - 1), 0)),
            _const_spec((CONV_WIDTH, 1, d)), _const_spec((1, d)), _const_spec((1, d)), _const_spec((1, d)),
        ],
        out_specs=pl.BlockSpec((None, tm, d), lambda b, i: (b, i, 0)),
        out_shape=jax.ShapeDtypeStruct((bsz, t, d), BF16),
        scratch_shapes=[pltpu.VMEM((tm + 2 * CONV_HALO, d), F32), pltpu.VMEM((tm, d), F32),
                        pltpu.VMEM((SUBLANES, tm + 2 * CONV_HALO - SUBLANES, LANE_BLOCK), F32)],
        compiler_params=_params(2),
        name="conv_module",
    )(u, u, u, w_taps, vec(b_dw), vec(ln_g), vec(ln_b))


def _ffn_kernel(*refs, n_act):
    x_ref = refs[0]
    act_refs = refs[1:1 + n_act]
    wo_refs = refs[1 + n_act:1 + 2 * n_act]
    (bo_ref, g1_ref, sh2_ref, sc2_ref, g2_ref, l1g_ref, l1b_ref, l2g_ref, l2b_ref,
     w1_ref, w3_ref, w2_ref, o_ref) = refs[1 + 2 * n_act:]
    y = bo_ref[...]
    for a_ref, w_ref in zip(act_refs, wo_refs):
        y = y + _dot(a_ref[...], w_ref[...])
    x1 = _layer_norm(ALPHA * x_ref[...] + g1_ref[...] * y, l1g_ref[...], l1b_ref[...])
    h = (x1 * (1.0 + sc2_ref[...]) + sh2_ref[...]).astype(BF16)
    u = (_silu(_dot(h, w1_ref[...])) * _dot(h, w3_ref[...])).astype(BF16)
    f = _dot(u, w2_ref[...])
    o_ref[...] = _layer_norm(ALPHA * x1 + g2_ref[...] * f, l2g_ref[...], l2b_ref[...])


def _out_ffn(x, acts, w_outs, b_out, mods, ln, w1, w3, w2, tm):
    bsz, t, d = x.shape
    n_act = len(acts)
    tok = lambda width: pl.BlockSpec((None, tm, width), lambda b, i: (b, i, 0))
    vec = lambda a: a.astype(F32).reshape(1, d)
    g1, sh2, sc2, g2 = mods
    in_specs = ([tok(d)] + [tok(a.shape[-1]) for a in acts] + [_const_spec(w.shape) for w in w_outs]
                + [_const_spec((1, d))] + [_mod_spec(m) for m in mods] + [_const_spec((1, d))] * 4
                + [_const_spec(w1.shape), _const_spec(w3.shape), _const_spec(w2.shape)])
    return pl.pallas_call(
        functools.partial(_ffn_kernel, n_act=n_act),
        grid=(bsz, t // tm),
        in_specs=in_specs,
        out_specs=tok(d),
        out_shape=jax.ShapeDtypeStruct((bsz, t, d), F32),
        compiler_params=_params(2),
        name="out_ffn",
    )(x, *acts, *w_outs, vec(b_out), g1, sh2, sc2, g2, *[vec(a) for a in ln], w1, w3, w2)


def kernel(x, c, ctx, c_ctx, w_mod, b_mod, ln1_g, ln1_b, ln2_g, ln2_b, w_in_even, w_out_even, na_rpb, hg_lb,
           hg_norm_g, cv_w1, cv_b1, cv_wdw, cv_bdw, cv_ln_g, cv_ln_b, cv_w2, cv_b2, ffn_w1, ffn_w3, ffn_w2):
    bsz, t, d = x.shape
    lc = ctx.shape[1]
    depth = w_mod.shape[0]
    last_even = depth - 1 if (depth - 1) % 2 == 0 else depth - 2
    tm = min(512, t)
    tmc = min(512, lc)

    p_lb = jax.nn.softmax(hg_lb.astype(F32), axis=1)
    hg_lower = jnp.maximum(jnp.cumsum(p_lb, axis=1) - p_lb[:, :1], 0.0)

    c_all = jnp.zeros((MOD_ROWS, d), F32).at[:bsz].set(c).at[bsz].set(c_ctx)
    mod_all = _modulation(c_all, w_mod, b_mod)

    xc = ctx
    for l in range(depth):
        ctx_used = l <= last_even
        ctx_out = l < last_even
        mod = [mod_all[l, :bsz, k * d:(k + 1) * d].reshape(bsz, 1, d) for k in range(6)]
        mod_c = [mod_all[l, bsz:bsz + 1, k * d:(k + 1) * d].reshape(1, 1, d) for k in range(6)]
        ln = (ln1_g[l], ln1_b[l], ln2_g[l], ln2_b[l])
        w1, w3, w2 = ffn_w1[l].astype(BF16), ffn_w3[l].astype(BF16), ffn_w2[l].astype(BF16)
        j = l // 2
        if l % 2 == 0:
            w_in = w_in_even[j].astype(BF16)
            w_out = w_out_even[j].astype(BF16)
            w_outs = [w_out[:NA_DIM], w_out[NA_DIM:]]
            b_out = jnp.zeros((d,), F32)
            qa, ka, va, qb, ib, zf, zb, gb = _inproj_even(x, mod[0], mod[1], w_in, tm)
            qa_c, ka_c, va_c, qb_c, ib_c, zf_c, zb_c, gb_c = _inproj_even(xc, mod_c[0], mod_c[1], w_in, tmc)
            o_na = _neighbourhood_attention(qa, ka, va, ka_c, va_c, na_rpb[j])
            o_hg, o_hg_c = _hgrn2((qb, ib, zf, zb, gb), (qb_c, ib_c, zf_c, zb_c, gb_c), hg_lower[:, j],
                                  hg_norm_g[j], ctx_out)
            acts = [o_na, o_hg]
            if ctx_out:
                acts_c = [_context_attention(qa_c, ka_c, va_c), o_hg_c]
        else:
            w_outs = [cv_w2[j].astype(BF16)]
            b_out = cv_b2[j]
            conv = (cv_wdw[j], cv_bdw[j], cv_ln_g[j], cv_ln_b[j])
            w_cv1 = cv_w1[j].astype(BF16)
            b_cv1 = cv_b1[j].astype(F32).reshape(1, 2 * d)
            acts = [_conv_module_mid(_inproj_odd(x, mod[0], mod[1], w_cv1, b_cv1, tm), *conv, min(256, t))]
            if ctx_out:
                acts_c = [_conv_module_mid(_inproj_odd(xc, mod_c[0], mod_c[1], w_cv1, b_cv1, tmc), *conv,
                                           min(256, lc))]
        x = _out_ffn(x, acts, w_outs, b_out, mod[2:], ln, w1, w3, w2, tm)
        if ctx_out:
            xc = _out_ffn(xc, acts_c, w_outs, b_out, mod_c[2:], ln, w1, w3, w2, tmc)
    return x
```

```python
import functools

import numpy as np
import jax
import jax.numpy as jnp
from jax import lax
from jax.experimental import pallas as pl
from jax.experimental.pallas import tpu as pltpu

F32 = jnp.float32
BF16 = jnp.bfloat16

D_MODEL = 1024
DEPTH = 4
GRID_W = 64
NA_HEADS = 8
NA_HEAD_DIM = 64
NA_DIM = NA_HEADS * NA_HEAD_DIM
NA_WIN_H = 8
NA_WIN_W = 16
HG_HEADS = 4
HG_HEAD_DIM = 128
HG_DIM = HG_HEADS * HG_HEAD_DIM
HG_CHUNK = 64
HG_LEVELS = 6
HG_GROUP = 8
SUBLANES = 8
HG_MATMUL_LEVEL = 4
LOG2_E = 1.4426950408889634
CONV_WIDTH = 31
CONV_HALO = 16
D_FF = 2816
ALPHA = (2 * DEPTH) ** 0.25
LN_EPS = 1e-5
RMS_EPS = 1e-6

VMEM_LIMIT_BYTES = 56 * 1024 * 1024
MOD_ROWS = 24


def _params(n_grid, vmem=VMEM_LIMIT_BYTES):
    return pltpu.CompilerParams(dimension_semantics=("arbitrary",) * n_grid, vmem_limit_bytes=vmem)


def _const_spec(shape):
    nd = len(shape)
    return pl.BlockSpec(shape, lambda *_: (0,) * nd, pipeline_mode=pl.Buffered(1))


def _layer_norm(x, g, b):
    mu = jnp.mean(x, axis=-1, keepdims=True)
    xc = x - mu
    var = jnp.mean(xc * xc, axis=-1, keepdims=True)
    return xc * lax.rsqrt(var + LN_EPS) * g + b


def _silu(x):
    return x * jax.nn.sigmoid(x)


def _dot(a, b):
    return jnp.dot(a, b, preferred_element_type=F32)


def _dot_nt(a, b):
    return lax.dot_general(a, b, (((1,), (1,)), ((), ())), preferred_element_type=F32)


def _dot_tn(a, b):
    return lax.dot_general(a, b, (((0,), (0,)), ((), ())), preferred_element_type=F32)


def _mod_kernel(c_ref, w_ref, b_ref, o_ref):
    s = _silu(c_ref[...]).astype(BF16)
    o_ref[...] = _dot(s, w_ref[...].astype(BF16)) + b_ref[...]


def _modulation(c_all, w_mod, b_mod):
    depth, d, n = w_mod.shape
    tn = 1536
    return pl.pallas_call(
        _mod_kernel,
        grid=(depth, n // tn),
        in_specs=[
            pl.BlockSpec((MOD_ROWS, d), lambda l, j: (0, 0)),
            pl.BlockSpec((None, d, tn), lambda l, j: (l, 0, j)),
            pl.BlockSpec((None, 1, tn), lambda l, j: (l, 0, j)),
        ],
        out_specs=pl.BlockSpec((None, MOD_ROWS, tn), lambda l, j: (l, 0, j)),
        out_shape=jax.ShapeDtypeStruct((depth, MOD_ROWS, n), F32),
        compiler_params=_params(2),
        name="modulation",
    )(c_all, w_mod, b_mod.reshape(depth, 1, n))


def _mod_spec(mod):
    per_batch = 1 if mod.shape[0] > 1 else 0
    return pl.BlockSpec((None, 1, D_MODEL), lambda b, i: (b * per_batch, 0, 0))


def _inproj_even_kernel(x_ref, sh_ref, sc_ref, w_ref, qa_ref, ka_ref, va_ref, qb_ref, ib_ref, zf_ref,
                        zb_ref, gb_ref):
    h = (x_ref[...] * (1.0 + sc_ref[...]) + sh_ref[...]).astype(BF16)
    outs = (qa_ref, ka_ref, va_ref, qb_ref, ib_ref, zf_ref, zb_ref, gb_ref)
    for s, o_ref in enumerate(outs):
        y = _dot(h, w_ref[:, s * NA_DIM:(s + 1) * NA_DIM])
        if s == 0:
            y = y * (NA_HEAD_DIM ** -0.5)
        o_ref[...] = y.astype(o_ref.dtype)


def _inproj_even(x, sh, sc, w_bf16, tm):
    bsz, t, d = x.shape
    tok = pl.BlockSpec((None, tm, d), lambda b, i: (b, i, 0))
    out_spec = pl.BlockSpec((None, tm, NA_DIM), lambda b, i: (b, i, 0))
    dtypes = (BF16, BF16, BF16, F32, BF16, F32, F32, F32)
    return pl.pallas_call(
        _inproj_even_kernel,
        grid=(bsz, t // tm),
        in_specs=[tok, _mod_spec(sh), _mod_spec(sc), _const_spec(w_bf16.shape)],
        out_specs=[out_spec] * 8,
        out_shape=[jax.ShapeDtypeStruct((bsz, t, NA_DIM), dt) for dt in dtypes],
        compiler_params=_params(2),
        name="inproj_even",
    )(x, sh, sc, w_bf16)


def _inproj_odd_kernel(x_ref, sh_ref, sc_ref, w_ref, b_ref, u_ref):
    h = (x_ref[...] * (1.0 + sc_ref[...]) + sh_ref[...]).astype(BF16)
    a = _dot(h, w_ref[:, :D_MODEL]) + b_ref[:, :D_MODEL]
    g = _dot(h, w_ref[:, D_MODEL:]) + b_ref[:, D_MODEL:]
    u_ref[...] = a * jax.nn.sigmoid(g)


def _inproj_odd(x, sh, sc, w_bf16, bias, tm):
    bsz, t, d = x.shape
    tok = pl.BlockSpec((None, tm, d), lambda b, i: (b, i, 0))
    return pl.pallas_call(
        _inproj_odd_kernel,
        grid=(bsz, t // tm),
        in_specs=[tok, _mod_spec(sh), _mod_spec(sc), _const_spec(w_bf16.shape), _const_spec(bias.shape)],
        out_specs=tok,
        out_shape=jax.ShapeDtypeStruct((bsz, t, d), F32),
        compiler_params=_params(2),
        name="inproj_odd",
    )(x, sh, sc, w_bf16, bias)


HEADS_PER_GROUP = 4
GROUP_W = HEADS_PER_GROUP * NA_HEAD_DIM
NA_TILE_ROWS = 4


def _softmax_pv(s_parts, v_parts):
    m = s_parts[0].max(axis=-1, keepdims=True)
    for s in s_parts[1:]:
        m = jnp.maximum(m, s.max(axis=-1, keepdims=True))
    ps = [jnp.exp(s - m) for s in s_parts]
    l = ps[0].sum(axis=-1, keepdims=True)
    for p in ps[1:]:
        l = l + p.sum(axis=-1, keepdims=True)
    inv = 1.0 / l
    o = _dot((ps[0] * inv).astype(BF16), v_parts[0])
    for p, v in zip(ps[1:], v_parts[1:]):
        o = o + _dot((p * inv).astype(BF16), v)
    return o


def _na_kernel(q_ref, k_ref, v_ref, kc_ref, vc_ref, bias_ref, o_ref, *, rows, key_rows):
    r0 = pl.program_id(1) * NA_TILE_ROWS
    first_row = jnp.clip(r0 - NA_WIN_H // 2, 0, rows - key_rows)
    start = pl.multiple_of(first_row * GRID_W, GRID_W)
    nq = q_ref.shape[0]
    nk = key_rows * GRID_W
    lane = lax.broadcasted_iota(jnp.int32, (nq, GROUP_W), 1)
    for g in range(NA_HEADS // HEADS_PER_GROUP):
        cols = slice(g * GROUP_W, (g + 1) * GROUP_W)
        qg = q_ref[:, cols]
        kg = k_ref[pl.ds(start, nk), cols]
        vg = v_ref[pl.ds(start, nk), cols]
        kcg = kc_ref[:, cols]
        vcg = vc_ref[:, cols]
        acc = jnp.zeros((nq, GROUP_W), F32)
        for hh in range(HEADS_PER_GROUP):
            head_lanes = (lane >= hh * NA_HEAD_DIM) & (lane < (hh + 1) * NA_HEAD_DIM)
            qh = jnp.where(head_lanes, qg, jnp.zeros_like(qg))
            s_loc = _dot_nt(qh, kg) + bias_ref[g * HEADS_PER_GROUP + hh]
            s_ctx = _dot_nt(qh, kcg)
            o = _softmax_pv([s_loc, s_ctx], [vg, vcg])
            acc = jnp.where(head_lanes, o, acc)
        o_ref[:, cols] = acc.astype(o_ref.dtype)


def _na_tile_geometry(rows):
    kr = min(NA_WIN_H, rows)
    key_rows = min(NA_TILE_ROWS + NA_WIN_H, rows)
    outside = 2 * NA_WIN_H - 1
    geo = []
    for r0 in range(0, rows, NA_TILE_ROWS):
        first = int(np.clip(r0 - NA_WIN_H // 2, 0, rows - key_rows))
        idx = np.full((NA_TILE_ROWS, key_rows), outside, np.int32)
        for i in range(NA_TILE_ROWS):
            r = r0 + i
            rs = int(np.clip(r - kr // 2, 0, rows - kr))
            assert first <= rs and rs + kr <= first + key_rows
            for j in range(rs - first, rs - first + kr):
                idx[i, j] = first + j - r + NA_WIN_H - 1
        geo.append(idx)
    return key_rows, geo


def _na_bias_table(rpb, rows):
    key_rows, geo = _na_tile_geometry(rows)
    assert all(np.array_equal(g, geo[1]) for g in geo[1:-1])
    col = np.arange(GRID_W)
    col_start = np.clip(col - NA_WIN_W // 2, 0, GRID_W - NA_WIN_W)
    col_in = (col[None, :] >= col_start[:, None]) & (col[None, :] < col_start[:, None] + NA_WIN_W)
    dc_idx = np.clip(col[None, :] - col[:, None] + NA_WIN_W - 1, 0, 2 * NA_WIN_W - 2)
    per_dr = jnp.where(col_in[None, None], rpb[:, :, dc_idx].astype(F32), -jnp.inf)
    per_dr = jnp.concatenate([per_dr, jnp.full_like(per_dr[:, :1], -jnp.inf)], axis=1)
    tabs = []
    for idx in (geo[0], geo[1], geo[-1]):
        blocks = per_dr[:, idx]
        tabs.append(jnp.transpose(blocks, (0, 1, 3, 2, 4)).reshape(
            NA_HEADS, NA_TILE_ROWS * GRID_W, key_rows * GRID_W))
    return key_rows, jnp.stack(tabs)


def _neighbourhood_attention(q, k, v, kc, vc, rpb):
    bsz, t, _ = q.shape
    lc = kc.shape[1]
    rows = t // GRID_W
    n_tiles = rows // NA_TILE_ROWS
    assert n_tiles >= 3
    key_rows, bias = _na_bias_table(rpb, rows)
    nq = NA_TILE_ROWS * GRID_W

    def bias_index(b, i):
        return (jnp.minimum(i, 1) + (i == n_tiles - 1).astype(jnp.int32), 0, 0, 0)

    tile_spec = pl.BlockSpec((None, nq, NA_DIM), lambda b, i: (b, i, 0))
    seq_spec = pl.BlockSpec((None, t, NA_DIM), lambda b, i: (b, 0, 0))
    ctx_spec = pl.BlockSpec((None, lc, NA_DIM), lambda b, i: (b, 0, 0))
    return pl.pallas_call(
        functools.partial(_na_kernel, rows=rows, key_rows=key_rows),
        grid=(bsz, n_tiles),
        in_specs=[tile_spec, seq_spec, seq_spec, ctx_spec, ctx_spec,
                  pl.BlockSpec((None, NA_HEADS, nq, key_rows * GRID_W), bias_index)],
        out_specs=tile_spec,
        out_shape=jax.ShapeDtypeStruct((bsz, t, NA_DIM), BF16),
        compiler_params=_params(2),
        name="neighbourhood_attention",
    )(q, k, v, kc, vc, bias)


def _ctx_attn_kernel(q_ref, k_ref, v_ref, o_ref):
    nq = q_ref.shape[0]
    lane = lax.broadcasted_iota(jnp.int32, (nq, GROUP_W), 1)
    for g in range(NA_HEADS // HEADS_PER_GROUP):
        cols = slice(g * GROUP_W, (g + 1) * GROUP_W)
        qg = q_ref[:, cols]
        kg = k_ref[:, cols]
        vg = v_ref[:, cols]
        acc = jnp.zeros((nq, GROUP_W), F32)
        for hh in range(HEADS_PER_GROUP):
            head_lanes = (lane >= hh * NA_HEAD_DIM) & (lane < (hh + 1) * NA_HEAD_DIM)
            qh = jnp.where(head_lanes, qg, jnp.zeros_like(qg))
            o = _softmax_pv([_dot_nt(qh, kg)], [vg])
            acc = jnp.where(head_lanes, o, acc)
        o_ref[:, cols] = acc.astype(o_ref.dtype)


def _context_attention(q, k, v):
    bsz, lc, _ = q.shape
    spec = pl.BlockSpec((None, lc, NA_DIM), lambda b: (b, 0, 0))
    return pl.pallas_call(
        _ctx_attn_kernel,
        grid=(bsz,),
        in_specs=[spec, spec, spec],
        out_specs=spec,
        out_shape=jax.ShapeDtypeStruct((bsz, lc, NA_DIM), BF16),
        compiler_params=_params(1),
        name="context_attention",
    )(q, k, v)


def _hgrn_tables():
    L = HG_CHUNK
    t = np.arange(L)
    p_rows, masks = [], []
    for lev in range(HG_LEVELS):
        c = L >> lev
        mid = (t // c) * c + c // 2
        P = np.zeros((L, L), np.float32)
        for i in range(L):
            m = mid[i]
            if i >= m:
                P[i, m + 1:i + 1] = 1.0
            else:
                P[i, i + 1:m + 1] = 1.0
        p_rows.append(P)
        same = (t[:, None] // c) == (t[None, :] // c)
        masks.append((same & (t[:, None] >= mid[:, None]) & (t[None, :] < mid[None, :])).astype(np.float32))
    incl = np.tril(np.ones((L, L), np.float32))
    fine = p_rows[HG_MATMUL_LEVEL:]
    pf = np.concatenate([incl] + fine, axis=0)
    flip = lambda a: a[..., ::-1, ::-1]
    pb = np.concatenate([flip(incl)] + [flip(p) for p in fine], axis=0)
    side_by_side = lambda ms: np.stack([np.concatenate([m, m], axis=1) for m in ms])
    mf = side_by_side(masks)
    mb = side_by_side([flip(m) for m in masks])
    return pf, pb, mf, mb


def _hgrn_gates(z, log_lb, log1m_lb, one_m_lb):
    e = jnp.exp(-jnp.abs(z))
    inv = 1.0 / (1.0 + e)
    key = one_m_lb * jnp.where(z >= 0, e * inv, inv)
    c = log1m_lb + (jnp.minimum(z, 0.0) - jnp.log1p(e))
    log_f = jnp.maximum(log_lb, c) + jnp.log1p(jnp.exp(-jnp.abs(log_lb - c)))
    return key, log_f


def _hgrn_group(q_ref, v_ref, z_ref, first_chunk, G, lbp, p_ref, m_ref, st_ref, reverse, emit):
    L, dk = HG_CHUNK, HG_HEAD_DIM
    base = pl.multiple_of(first_chunk * L, G * L)
    group_rows = pl.ds(base, G * L)
    chunk = lambda a, j: a[j * L:(j + 1) * L]
    wide = lambda a: jnp.concatenate([chunk(a, j) for j in range(G)], axis=1)

    q, v, z = q_ref[group_rows, :], v_ref[group_rows, :], z_ref[group_rows, :]
    key, log_f = _hgrn_gates(z, lbp[0], lbp[1], lbp[2])
    log2_f = log_f * LOG2_E
    g_hi = log2_f.astype(BF16)
    g_lo = (log2_f - g_hi.astype(F32)).astype(BF16)
    p = p_ref[...]
    x = _dot(p, wide(g_hi)) + _dot(p, wide(g_lo))
    cum = x[:L]

    def level_exponent(c):
        pieces = []
        for i in range(L // c):
            mid = i * c + c // 2 - (1 if reverse else 0)
            ref = cum[mid:mid + 1]
            blk = cum[i * c:(i + 1) * c]
            if c >= 2 * SUBLANES:
                first_half, second_half = blk[:c // 2], blk[c // 2:]
                pieces += ([first_half - ref, ref - second_half] if reverse
                           else [ref - first_half, second_half - ref])
            else:
                pieces.append(-jnp.abs(blk - ref))
        return jnp.concatenate(pieces, axis=0)

    last_row = 0 if reverse else L - 1
    exponents = ([level_exponent(L >> lev) for lev in range(HG_MATMUL_LEVEL)]
                 + [x[(1 + i) * L:(2 + i) * L] for i in range(HG_LEVELS - HG_MATMUL_LEVEL)]
                 + [cum, cum[last_row:last_row + 1] - cum])
    decay = [jnp.exp2(e) for e in exponents]
    decay16 = [d.astype(BF16) for d in decay]
    dec16 = lambda blk, j: decay16[blk][:, j * dk:(j + 1) * dk]
    q16, key16 = q.astype(BF16), key.astype(BF16)
    qs = [chunk(q16, j) for j in range(G)]
    ks = [chunk(key16, j) for j in range(G)]
    vs = [chunk(v, j) for j in range(G)]
    qk = jnp.sum(q * key, axis=-1, keepdims=True)

    zero = jnp.zeros((L, dk), BF16)
    block_diag = lambda a, b: jnp.concatenate([jnp.concatenate([a, zero], axis=1),
                                               jnp.concatenate([zero, b], axis=1)], axis=0)
    row = lax.broadcasted_iota(jnp.int32, (L, 2 * L), 0)
    col = lax.broadcasted_iota(jnp.int32, (L, 2 * L), 1)
    pairs = range(0, G, 2)
    a = [jnp.where(col == row, chunk(qk, j), jnp.where(col == row + L, chunk(qk, j + 1), 0.0)) for j in pairs]
    for lev in range(HG_LEVELS):
        for i, j in enumerate(pairs):
            e0, e1 = dec16(lev, j), dec16(lev, j + 1)
            q_pair = jnp.concatenate([qs[j] * e0, qs[j + 1] * e1], axis=1)
            a[i] = a[i] + m_ref[lev] * _dot_nt(q_pair, block_diag(ks[j] * e0, ks[j + 1] * e1))
    o_local = []
    for i, j in enumerate(pairs):
        o_pair = _dot(a[i].astype(BF16), block_diag(vs[j], vs[j + 1]))
        o_local += [o_pair[:, :dk], o_pair[:, dk:]]
    q_in = [qs[j] * dec16(HG_LEVELS, j) for j in range(G)]
    st_add = [_dot_tn(vs[j], ks[j] * dec16(HG_LEVELS + 1, j)) for j in range(G)]
    e_all = [decay[HG_LEVELS][last_row:last_row + 1, j * dk:(j + 1) * dk] for j in range(G)]

    st = st_ref[...]
    for j in (reversed(range(G)) if reverse else range(G)):
        emit(pl.ds(base + j * L, L), o_local[j] + _dot_nt(q_in[j], st.astype(BF16)))
        st = st * e_all[j] + st_add[j]
    st_ref[...] = st


def _hgrn_kernel(q_ref, v_ref, zf_ref, zb_ref, g_ref, qc_ref, vc_ref, zfc_ref, zbc_ref, gc_ref,
                 lb_ref, ng_ref, pf_ref, pb_ref, mf_ref, mb_ref, *rest, ctx_out):
    if ctx_out:
        o_ref, oc_ref, st_ref, of_ref, ofc_ref = rest
    else:
        o_ref, st_ref, of_ref = rest
        oc_ref = ofc_ref = None
    norm_g = ng_ref[...]

    def readout(o, g):
        o = o * lax.rsqrt(jnp.mean(o * o, axis=-1, keepdims=True) + RMS_EPS) * norm_g
        return (o * _silu(g)).astype(BF16)

    def store_to(dst_ref):
        def emit(rows, o):
            dst_ref[rows, :] = o
        return emit

    def readout_to(dst_ref, fwd_ref, gate_ref):
        def emit(rows, o):
            dst_ref[rows, :] = readout(fwd_ref[rows, :] + o, gate_ref[rows, :])
        return emit

    def drop(rows, o):
        pass

    def scan(refs, lbp, p_ref, m_ref, reverse, emit):
        n_chunks = refs[0].shape[0] // HG_CHUNK
        group = min(HG_GROUP, n_chunks)
        n_groups = n_chunks // group

        def body(i, carry):
            first_chunk = ((n_groups - 1 - i) if reverse else i) * group
            _hgrn_group(*refs, first_chunk, group, lbp, p_ref, m_ref, st_ref, reverse, emit)
            return carry
        lax.fori_loop(0, n_groups, body, 0)

    lbf = (lb_ref[0, 0], lb_ref[0, 1], lb_ref[0, 2])
    st_ref[...] = jnp.zeros_like(st_ref)
    scan((qc_ref, vc_ref, zfc_ref), lbf, pf_ref, mf_ref, False, store_to(ofc_ref) if ctx_out else drop)
    scan((q_ref, v_ref, zf_ref), lbf, pf_ref, mf_ref, False, store_to(of_ref))

    lbb = (lb_ref[1, 0], lb_ref[1, 1], lb_ref[1, 2])
    st_ref[...] = jnp.zeros_like(st_ref)
    scan((qc_ref, vc_ref, zbc_ref), lbb, pb_ref, mb_ref, True,
         readout_to(oc_ref, ofc_ref, gc_ref) if ctx_out else drop)
    scan((q_ref, v_ref, zb_ref), lbb, pb_ref, mb_ref, True, readout_to(o_ref, of_ref, g_ref))


def _hgrn2(main, ctx, lb, norm_g, ctx_out):
    q = main[0]
    bsz, t, _ = q.shape
    lc = ctx[0].shape[1]
    pf, pb, mf, mb = _hgrn_tables()
    lb = lb.astype(F32)
    lb_params = jnp.stack([jnp.log(lb), jnp.log1p(-lb), 1.0 - lb], axis=1)
    lb_params = jnp.transpose(lb_params, (2, 0, 1, 3))[:, :, :, None, :]
    seq_spec = pl.BlockSpec((None, t, HG_HEAD_DIM), lambda b, h: (b, 0, h))
    ctx_spec = pl.BlockSpec((None, lc, HG_HEAD_DIM), lambda b, h: (b, 0, h))
    in_specs = ([seq_spec] * 5 + [ctx_spec] * 5 + [
        pl.BlockSpec((None, 2, 3, 1, HG_HEAD_DIM), lambda b, h: (h, 0, 0, 0, 0)),
        pl.BlockSpec((None, 1, HG_HEAD_DIM), lambda b, h: (h, 0, 0)),
        _const_spec(pf.shape), _const_spec(pb.shape), _const_spec(mf.shape), _const_spec(mb.shape)])
    out_specs = [seq_spec]
    out_shape = [jax.ShapeDtypeStruct((bsz, t, HG_DIM), BF16)]
    scratch = [pltpu.VMEM((HG_HEAD_DIM, HG_HEAD_DIM), F32), pltpu.VMEM((t, HG_HEAD_DIM), F32)]
    if ctx_out:
        out_specs.append(ctx_spec)
        out_shape.append(jax.ShapeDtypeStruct((bsz, lc, HG_DIM), BF16))
        scratch.append(pltpu.VMEM((lc, HG_HEAD_DIM), F32))
    outs = pl.pallas_call(
        functools.partial(_hgrn_kernel, ctx_out=ctx_out),
        grid=(bsz, HG_HEADS),
        in_specs=in_specs,
        out_specs=out_specs,
        out_shape=out_shape,
        scratch_shapes=scratch,
        compiler_params=_params(2),
        name="hgrn2",
    )(*main, *ctx, lb_params, norm_g.astype(F32)[:, None, :],
      jnp.asarray(pf, BF16), jnp.asarray(pb, BF16), jnp.asarray(mf), jnp.asarray(mb))
    return (outs[0], outs[1]) if ctx_out else (outs[0], None)


LANE_BLOCK = 128


def _conv_kernel(prev_ref, cur_ref, next_ref, w_ref, b_ref, g_ref, beta_ref, o_ref, win_ref, acc_ref,
                 shift_ref):
    i = pl.program_id(1)
    tm = cur_ref.shape[0]
    win_ref[0:CONV_HALO, :] = jnp.where(i > 0, prev_ref[...], 0.0)
    win_ref[CONV_HALO:CONV_HALO + tm, :] = cur_ref[...]
    win_ref[CONV_HALO + tm:, :] = jnp.where(i < pl.num_programs(1) - 1, next_ref[...], 0.0)
    first = CONV_HALO - CONV_WIDTH // 2
    span = tm + 2 * CONV_HALO - SUBLANES
    for c in range(D_MODEL // LANE_BLOCK):
        cols = slice(c * LANE_BLOCK, (c + 1) * LANE_BLOCK)
        for phase in range(SUBLANES):
            shift_ref[phase] = win_ref[phase:phase + span, cols]
        acc = jnp.zeros((tm, LANE_BLOCK), F32)
        for k in range(CONV_WIDTH):
            phase = (first + k) % SUBLANES
            off = first + k - phase
            acc = acc + shift_ref[phase, off:off + tm, :] * w_ref[k, :, cols]
        acc_ref[:, cols] = acc + b_ref[:, cols]
    y = _layer_norm(acc_ref[...], g_ref[...], beta_ref[...])
    o_ref[...] = _silu(y).astype(o_ref.dtype)


def _conv_module_mid(u, w_dw, b_dw, ln_g, ln_b, tm):
    bsz, t, d = u.shape
    hb = tm // CONV_HALO
    n_hb = t // CONV_HALO
    vec = lambda a: a.astype(F32).reshape(1, d)
    w_taps = w_dw.astype(F32).reshape(CONV_WIDTH, 1, d)
    return pl.pallas_call(
        _conv_kernel,
        grid=(bsz, t // tm),
        in_specs=[
            pl.BlockSpec((None, CONV_HALO, d), lambda b, i: (b, jnp.maximum(i * hb - 1, 0), 0)),
            pl.BlockSpec((None, tm, d), lambda b, i: (b, i, 0)),
            pl.BlockSpec((None, CONV_HALO, d), lambda b, i: (b, jnp.minimum((i + 1) * hb, n_hb - 1), 0)),
            _const_spec((CONV_WIDTH, 1, d)), _const_spec((1, d)), _const_spec((1, d)), _const_spec((1, d)),
        ],
        out_specs=pl.BlockSpec((None, tm, d), lambda b, i: (b, i, 0)),
        out_shape=jax.ShapeDtypeStruct((bsz, t, d), BF16),
        scratch_shapes=[pltpu.VMEM((tm + 2 * CONV_HALO, d), F32), pltpu.VMEM((tm, d), F32),
                        pltpu.VMEM((SUBLANES, tm + 2 * CONV_HALO - SUBLANES, LANE_BLOCK), F32)],
        compiler_params=_params(2),
        name="conv_module",
    )(u, u, u, w_taps, vec(b_dw), vec(ln_g), vec(ln_b))


def _ffn_kernel(*refs, n_act):
    x_ref = refs[0]
    act_refs = refs[1:1 + n_act]
    wo_refs = refs[1 + n_act:1 + 2 * n_act]
    (bo_ref, g1_ref, sh2_ref, sc2_ref, g2_ref, l1g_ref, l1b_ref, l2g_ref, l2b_ref,
     w1_ref, w3_ref, w2_ref, o_ref) = refs[1 + 2 * n_act:]
    y = bo_ref[...]
    for a_ref, w_ref in zip(act_refs, wo_refs):
        y = y + _dot(a_ref[...], w_ref[...])
    x1 = _layer_norm(ALPHA * x_ref[...] + g1_ref[...] * y, l1g_ref[...], l1b_ref[...])
    h = (x1 * (1.0 + sc2_ref[...]) + sh2_ref[...]).astype(BF16)
    u = (_silu(_dot(h, w1_ref[...])) * _dot(h, w3_ref[...])).astype(BF16)
    f = _dot(u, w2_ref[...])
    o_ref[...] = _layer_norm(ALPHA * x1 + g2_ref[...] * f, l2g_ref[...], l2b_ref[...])


def _out_ffn(x, acts, w_outs, b_out, mods, ln, w1, w3, w2, tm):
    bsz, t, d = x.shape
    n_act = len(acts)
    tok = lambda width: pl.BlockSpec((None, tm, width), lambda b, i: (b, i, 0))
    vec = lambda a: a.astype(F32).reshape(1, d)
    g1, sh2, sc2, g2 = mods
    in_specs = ([tok(d)] + [tok(a.shape[-1]) for a in acts] + [_const_spec(w.shape) for w in w_outs]
                + [_const_spec((1, d))] + [_mod_spec(m) for m in mods] + [_const_spec((1, d))] * 4
                + [_const_spec(w1.shape), _const_spec(w3.shape), _const_spec(w2.shape)])
    return pl.pallas_call(
        functools.partial(_ffn_kernel, n_act=n_act),
        grid=(bsz, t // tm),
        in_specs=in_specs,
        out_specs=tok(d),
        out_shape=jax.ShapeDtypeStruct((bsz, t, d), F32),
        compiler_params=_params(2),
        name="out_ffn",
    )(x, *acts, *w_outs, vec(b_out), g1, sh2, sc2, g2, *[vec(a) for a in ln], w1, w3, w2)


def kernel(x, c, ctx, c_ctx, w_mod, b_mod, ln1_g, ln1_b, ln2_g, ln2_b, w_in_even, w_out_even, na_rpb, hg_lb,
           hg_norm_g, cv_w1, cv_b1, cv_wdw, cv_bdw, cv_ln_g, cv_ln_b, cv_w2, cv_b2, ffn_w1, ffn_w3, ffn_w2):
    bsz, t, d = x.shape
    lc = ctx.shape[1]
    depth = w_mod.shape[0]
    last_even = depth - 1 if (depth - 1) % 2 == 0 else depth - 2
    tm = min(512, t)
    tmc = min(512, lc)

    p_lb = jax.nn.softmax(hg_lb.astype(F32), axis=1)
    hg_lower = jnp.maximum(jnp.cumsum(p_lb, axis=1) - p_lb[:, :1], 0.0)

    c_all = jnp.zeros((MOD_ROWS, d), F32).at[:bsz].set(c).at[bsz].set(c_ctx)
    mod_all = _modulation(c_all, w_mod, b_mod)

    xc = ctx
    for l in range(depth):
        ctx_used = l <= last_even
        ctx_out = l < last_even
        mod = [mod_all[l, :bsz, k * d:(k + 1) * d].reshape(bsz, 1, d) for k in range(6)]
        mod_c = [mod_all[l, bsz:bsz + 1, k * d:(k + 1) * d].reshape(1, 1, d) for k in range(6)]
        ln = (ln1_g[l], ln1_b[l], ln2_g[l], ln2_b[l])
        w1, w3, w2 = ffn_w1[l].astype(BF16), ffn_w3[l].astype(BF16), ffn_w2[l].astype(BF16)
        j = l // 2
        if l % 2 == 0:
            w_in = w_in_even[j].astype(BF16)
            w_out = w_out_even[j].astype(BF16)
            w_outs = [w_out[:NA_DIM], w_out[NA_DIM:]]
            b_out = jnp.zeros((d,), F32)
            qa, ka, va, qb, ib, zf, zb, gb = _inproj_even(x, mod[0], mod[1], w_in, tm)
            qa_c, ka_c, va_c, qb_c, ib_c, zf_c, zb_c, gb_c = _inproj_even(xc, mod_c[0], mod_c[1], w_in, tmc)
            o_na = _neighbourhood_attention(qa, ka, va, ka_c, va_c, na_rpb[j])
            o_hg, o_hg_c = _hgrn2((qb, ib, zf, zb, gb), (qb_c, ib_c, zf_c, zb_c, gb_c), hg_lower[:, j],
                                  hg_norm_g[j], ctx_out)
            acts = [o_na, o_hg]
            if ctx_out:
                acts_c = [_context_attention(qa_c, ka_c, va_c), o_hg_c]
        else:
            w_outs = [cv_w2[j].astype(BF16)]
            b_out = cv_b2[j]
            conv = (cv_wdw[j], cv_bdw[j], cv_ln_g[j], cv_ln_b[j])
            w_cv1 = cv_w1[j].astype(BF16)
            b_cv1 = cv_b1[j].astype(F32).reshape(1, 2 * d)
            acts = [_conv_module_mid(_inproj_odd(x, mod[0], mod[1], w_cv1, b_cv1, tm), *conv, min(256, t))]
            if ctx_out:
                acts_c = [_conv_module_mid(_inproj_odd(xc, mod_c[0], mod_c[1], w_cv1, b_cv1, tmc), *conv,
                                           min(256, lc))]
        x = _out_ffn(x, acts, w_outs, b_out, mod[2:], ln, w1, w3, w2, tm)
        if ctx_out:
            xc = _out_ffn(xc, acts_c, w_outs, b_out, mod_c[2:], ln, w1, w3, w2, tmc)
    return x
```

```python
import functools

import numpy as np
import jax
import jax.numpy as jnp
from jax import lax
from jax.experimental import pallas as pl
from jax.experimental.pallas import tpu as pltpu

F32 = jnp.float32
BF16 = jnp.bfloat16

D_MODEL = 1024
DEPTH = 4
GRID_W = 64
NA_HEADS = 8
NA_HEAD_DIM = 64
NA_DIM = NA_HEADS * NA_HEAD_DIM
NA_WIN_H = 8
NA_WIN_W = 16
HG_HEADS = 4
HG_HEAD_DIM = 128
HG_DIM = HG_HEADS * HG_HEAD_DIM
HG_CHUNK = 64
HG_LEVELS = 6
HG_GROUP = 8
SUBLANES = 8
HG_MATMUL_LEVEL = 4
LOG2_E = 1.4426950408889634
CONV_WIDTH = 31
CONV_HALO = 16
D_FF = 2816
ALPHA = (2 * DEPTH) ** 0.25
LN_EPS = 1e-5
RMS_EPS = 1e-6

VMEM_LIMIT_BYTES = 56 * 1024 * 1024
MOD_ROWS = 24


def _params(n_grid, vmem=VMEM_LIMIT_BYTES):
    return pltpu.CompilerParams(dimension_semantics=("arbitrary",) * n_grid, vmem_limit_bytes=vmem)


def _const_spec(shape):
    nd = len(shape)
    return pl.BlockSpec(shape, lambda *_: (0,) * nd, pipeline_mode=pl.Buffered(1))


def _layer_norm(x, g, b):
    mu = jnp.mean(x, axis=-1, keepdims=True)
    xc = x - mu
    var = jnp.mean(xc * xc, axis=-1, keepdims=True)
    return xc * lax.rsqrt(var + LN_EPS) * g + b


def _silu(x):
    return x * jax.nn.sigmoid(x)


def _dot(a, b):
    return jnp.dot(a, b, preferred_element_type=F32)


def _dot_nt(a, b):
    return lax.dot_general(a, b, (((1,), (1,)), ((), ())), preferred_element_type=F32)


def _dot_tn(a, b):
    return lax.dot_general(a, b, (((0,), (0,)), ((), ())), preferred_element_type=F32)


def _mod_kernel(c_ref, w_ref, b_ref, o_ref):
    s = _silu(c_ref[...]).astype(BF16)
    o_ref[...] = _dot(s, w_ref[...].astype(BF16)) + b_ref[...]


def _modulation(c_all, w_mod, b_mod):
    depth, d, n = w_mod.shape
    tn = 1536
    return pl.pallas_call(
        _mod_kernel,
        grid=(depth, n // tn),
        in_specs=[
            pl.BlockSpec((MOD_ROWS, d), lambda l, j: (0, 0)),
            pl.BlockSpec((None, d, tn), lambda l, j: (l, 0, j)),
            pl.BlockSpec((None, 1, tn), lambda l, j: (l, 0, j)),
        ],
        out_specs=pl.BlockSpec((None, MOD_ROWS, tn), lambda l, j: (l, 0, j)),
        out_shape=jax.ShapeDtypeStruct((depth, MOD_ROWS, n), F32),
        compiler_params=_params(2),
        name="modulation",
    )(c_all, w_mod, b_mod.reshape(depth, 1, n))


def _mod_spec(mod):
    per_batch = 1 if mod.shape[0] > 1 else 0
    return pl.BlockSpec((None, 1, D_MODEL), lambda b, i: (b * per_batch, 0, 0))


def _inproj_even_kernel(x_ref, sh_ref, sc_ref, w_ref, qa_ref, ka_ref, va_ref, qb_ref, ib_ref, zf_ref,
                        zb_ref, gb_ref):
    h = (x_ref[...] * (1.0 + sc_ref[...]) + sh_ref[...]).astype(BF16)
    outs = (qa_ref, ka_ref, va_ref, qb_ref, ib_ref, zf_ref, zb_ref, gb_ref)
    for s, o_ref in enumerate(outs):
        y = _dot(h, w_ref[:, s * NA_DIM:(s + 1) * NA_DIM])
        if s == 0:
            y = y * (NA_HEAD_DIM ** -0.5 * LOG2_E)
        o_ref[...] = y.astype(o_ref.dtype)


def _inproj_even(x, sh, sc, w_bf16, tm):
    bsz, t, d = x.shape
    tok = pl.BlockSpec((None, tm, d), lambda b, i: (b, i, 0))
    out_spec = pl.BlockSpec((None, tm, NA_DIM), lambda b, i: (b, i, 0))
    dtypes = (BF16, BF16, BF16, F32, BF16, F32, F32, F32)
    return pl.pallas_call(
        _inproj_even_kernel,
        grid=(bsz, t // tm),
        in_specs=[tok, _mod_spec(sh), _mod_spec(sc), _const_spec(w_bf16.shape)],
        out_specs=[out_spec] * 8,
        out_shape=[jax.ShapeDtypeStruct((bsz, t, NA_DIM), dt) for dt in dtypes],
        compiler_params=_params(2),
        name="inproj_even",
    )(x, sh, sc, w_bf16)


def _inproj_odd_kernel(x_ref, sh_ref, sc_ref, w_ref, b_ref, u_ref):
    h = (x_ref[...] * (1.0 + sc_ref[...]) + sh_ref[...]).astype(BF16)
    a = _dot(h, w_ref[:, :D_MODEL]) + b_ref[:, :D_MODEL]
    g = _dot(h, w_ref[:, D_MODEL:]) + b_ref[:, D_MODEL:]
    u_ref[...] = a * jax.nn.sigmoid(g)


def _inproj_odd(x, sh, sc, w_bf16, bias, tm):
    bsz, t, d = x.shape
    tok = pl.BlockSpec((None, tm, d), lambda b, i: (b, i, 0))
    return pl.pallas_call(
        _inproj_odd_kernel,
        grid=(bsz, t // tm),
        in_specs=[tok, _mod_spec(sh), _mod_spec(sc), _const_spec(w_bf16.shape), _const_spec(bias.shape)],
        out_specs=tok,
        out_shape=jax.ShapeDtypeStruct((bsz, t, d), F32),
        compiler_params=_params(2),
        name="inproj_odd",
    )(x, sh, sc, w_bf16, bias)


HEADS_PER_GROUP = 4
GROUP_W = HEADS_PER_GROUP * NA_HEAD_DIM
NA_TILE_ROWS = 4


def _softmax_pv(s_parts, v_parts):
    m = s_parts[0].max(axis=-1, keepdims=True)
    for s in s_parts[1:]:
        m = jnp.maximum(m, s.max(axis=-1, keepdims=True))
    ps = [jnp.exp2(s - m) for s in s_parts]
    l = ps[0].sum(axis=-1, keepdims=True)
    for p in ps[1:]:
        l = l + p.sum(axis=-1, keepdims=True)
    o = _dot(ps[0].astype(BF16), v_parts[0])
    for p, v in zip(ps[1:], v_parts[1:]):
        o = o + _dot(p.astype(BF16), v)
    return o * (1.0 / l)


def _na_kernel(q_ref, k_ref, v_ref, kc_ref, vc_ref, bias_ref, o_ref, *, rows, key_rows):
    r0 = pl.program_id(1) * NA_TILE_ROWS
    first_row = jnp.clip(r0 - NA_WIN_H // 2, 0, rows - key_rows)
    start = pl.multiple_of(first_row * GRID_W, GRID_W)
    nq = q_ref.shape[0]
    nk = key_rows * GRID_W
    lane = lax.broadcasted_iota(jnp.int32, (nq, GROUP_W), 1)
    for g in range(NA_HEADS // HEADS_PER_GROUP):
        cols = slice(g * GROUP_W, (g + 1) * GROUP_W)
        qg = q_ref[:, cols]
        kg = k_ref[pl.ds(start, nk), cols]
        vg = v_ref[pl.ds(start, nk), cols]
        kcg = kc_ref[:, cols]
        vcg = vc_ref[:, cols]
        head_lanes = [(lane >= hh * NA_HEAD_DIM) & (lane < (hh + 1) * NA_HEAD_DIM)
                      for hh in range(HEADS_PER_GROUP)]
        scores = []
        for hh in range(HEADS_PER_GROUP):
            qh = jnp.where(head_lanes[hh], qg, jnp.zeros_like(qg))
            scores.append([_dot_nt(qh, kg) + bias_ref[g * HEADS_PER_GROUP + hh], _dot_nt(qh, kcg)])
        acc = jnp.zeros((nq, GROUP_W), F32)
        for hh in range(HEADS_PER_GROUP):
            acc = jnp.where(head_lanes[hh], _softmax_pv(scores[hh], [vg, vcg]), acc)
        o_ref[:, cols] = acc.astype(o_ref.dtype)


def _na_tile_geometry(rows):
    kr = min(NA_WIN_H, rows)
    key_rows = min(NA_TILE_ROWS + NA_WIN_H, rows)
    outside = 2 * NA_WIN_H - 1
    geo = []
    for r0 in range(0, rows, NA_TILE_ROWS):
        first = int(np.clip(r0 - NA_WIN_H // 2, 0, rows - key_rows))
        idx = np.full((NA_TILE_ROWS, key_rows), outside, np.int32)
        for i in range(NA_TILE_ROWS):
            r = r0 + i
            rs = int(np.clip(r - kr // 2, 0, rows - kr))
            assert first <= rs and rs + kr <= first + key_rows
            for j in range(rs - first, rs - first + kr):
                idx[i, j] = first + j - r + NA_WIN_H - 1
        geo.append(idx)
    return key_rows, geo


def _na_bias_table(rpb, rows):
    key_rows, geo = _na_tile_geometry(rows)
    assert all(np.array_equal(g, geo[1]) for g in geo[1:-1])
    col = np.arange(GRID_W)
    col_start = np.clip(col - NA_WIN_W // 2, 0, GRID_W - NA_WIN_W)
    col_in = (col[None, :] >= col_start[:, None]) & (col[None, :] < col_start[:, None] + NA_WIN_W)
    dc_idx = np.clip(col[None, :] - col[:, None] + NA_WIN_W - 1, 0, 2 * NA_WIN_W - 2)
    per_dr = jnp.where(col_in[None, None], rpb[:, :, dc_idx].astype(F32) * LOG2_E, -jnp.inf)
    per_dr = jnp.concatenate([per_dr, jnp.full_like(per_dr[:, :1], -jnp.inf)], axis=1)
    tabs = []
    for idx in (geo[0], geo[1], geo[-1]):
        blocks = per_dr[:, idx]
        tabs.append(jnp.transpose(blocks, (0, 1, 3, 2, 4)).reshape(
            NA_HEADS, NA_TILE_ROWS * GRID_W, key_rows * GRID_W))
    return key_rows, jnp.stack(tabs)


def _neighbourhood_attention(q, k, v, kc, vc, rpb):
    bsz, t, _ = q.shape
    lc = kc.shape[1]
    rows = t // GRID_W
    n_tiles = rows // NA_TILE_ROWS
    assert n_tiles >= 3
    key_rows, bias = _na_bias_table(rpb, rows)
    nq = NA_TILE_ROWS * GRID_W

    def bias_index(b, i):
        return (jnp.minimum(i, 1) + (i == n_tiles - 1).astype(jnp.int32), 0, 0, 0)

    tile_spec = pl.BlockSpec((None, nq, NA_DIM), lambda b, i: (b, i, 0))
    seq_spec = pl.BlockSpec((None, t, NA_DIM), lambda b, i: (b, 0, 0))
    ctx_spec = pl.BlockSpec((None, lc, NA_DIM), lambda b, i: (b, 0, 0))
    return pl.pallas_call(
        functools.partial(_na_kernel, rows=rows, key_rows=key_rows),
        grid=(bsz, n_tiles),
        in_specs=[tile_spec, seq_spec, seq_spec, ctx_spec, ctx_spec,
                  pl.BlockSpec((None, NA_HEADS, nq, key_rows * GRID_W), bias_index)],
        out_specs=tile_spec,
        out_shape=jax.ShapeDtypeStruct((bsz, t, NA_DIM), BF16),
        compiler_params=_params(2),
        name="neighbourhood_attention",
    )(q, k, v, kc, vc, bias)


def _ctx_attn_kernel(q_ref, k_ref, v_ref, o_ref):
    nq = q_ref.shape[0]
    lane = lax.broadcasted_iota(jnp.int32, (nq, GROUP_W), 1)
    for g in range(NA_HEADS // HEADS_PER_GROUP):
        cols = slice(g * GROUP_W, (g + 1) * GROUP_W)
        qg = q_ref[:, cols]
        kg = k_ref[:, cols]
        vg = v_ref[:, cols]
        acc = jnp.zeros((nq, GROUP_W), F32)
        for hh in range(HEADS_PER_GROUP):
            head_lanes = (lane >= hh * NA_HEAD_DIM) & (lane < (hh + 1) * NA_HEAD_DIM)
            qh = jnp.where(head_lanes, qg, jnp.zeros_like(qg))
            o = _softmax_pv([_dot_nt(qh, kg)], [vg])
            acc = jnp.where(head_lanes, o, acc)
        o_ref[:, cols] = acc.astype(o_ref.dtype)


def _context_attention(q, k, v):
    bsz, lc, _ = q.shape
    spec = pl.BlockSpec((None, lc, NA_DIM), lambda b: (b, 0, 0))
    return pl.pallas_call(
        _ctx_attn_kernel,
        grid=(bsz,),
        in_specs=[spec, spec, spec],
        out_specs=spec,
        out_shape=jax.ShapeDtypeStruct((bsz, lc, NA_DIM), BF16),
        compiler_params=_params(1),
        name="context_attention",
    )(q, k, v)


def _hgrn_tables():
    L = HG_CHUNK
    t = np.arange(L)
    p_rows, masks = [], []
    for lev in range(HG_LEVELS):
        c = L >> lev
        mid = (t // c) * c + c // 2
        P = np.zeros((L, L), np.float32)
        for i in range(L):
            m = mid[i]
            if i >= m:
                P[i, m + 1:i + 1] = 1.0
            else:
                P[i, i + 1:m + 1] = 1.0
        p_rows.append(P)
        same = (t[:, None] // c) == (t[None, :] // c)
        masks.append((same & (t[:, None] >= mid[:, None]) & (t[None, :] < mid[None, :])).astype(np.float32))
    incl = np.tril(np.ones((L, L), np.float32))
    fine = p_rows[HG_MATMUL_LEVEL:]
    pf = np.concatenate([incl] + fine, axis=0)
    flip = lambda a: a[..., ::-1, ::-1]
    pb = np.concatenate([flip(incl)] + [flip(p) for p in fine], axis=0)
    side_by_side = lambda ms: np.stack([np.concatenate([m, m], axis=1) for m in ms])
    mf = side_by_side(masks)
    mb = side_by_side([flip(m) for m in masks])
    return pf, pb, mf, mb


def _hgrn_gates(z, log_lb, log1m_lb, one_m_lb):
    e = jnp.exp(-jnp.abs(z))
    inv = 1.0 / (1.0 + e)
    key = one_m_lb * jnp.where(z >= 0, e * inv, inv)
    c = log1m_lb + (jnp.minimum(z, 0.0) - jnp.log1p(e))
    log_f = jnp.maximum(log_lb, c) + jnp.log1p(jnp.exp(-jnp.abs(log_lb - c)))
    return key, log_f


def _hgrn_group(q_ref, v_ref, z_ref, first_chunk, G, lbp, p_ref, m_ref, st_ref, reverse, emit):
    L, dk = HG_CHUNK, HG_HEAD_DIM
    base = pl.multiple_of(first_chunk * L, G * L)
    group_rows = pl.ds(base, G * L)
    chunk = lambda a, j: a[j * L:(j + 1) * L]
    wide = lambda a: jnp.concatenate([chunk(a, j) for j in range(G)], axis=1)

    q, v, z = q_ref[group_rows, :], v_ref[group_rows, :], z_ref[group_rows, :]
    key, log_f = _hgrn_gates(z, lbp[0], lbp[1], lbp[2])
    log2_f = log_f * LOG2_E
    g_hi = log2_f.astype(BF16)
    g_lo = (log2_f - g_hi.astype(F32)).astype(BF16)
    p = p_ref[...]
    x = _dot(p, wide(g_hi)) + _dot(p, wide(g_lo))
    cum = x[:L]

    def level_exponent(c):
        pieces = []
        for i in range(L // c):
            mid = i * c + c // 2 - (1 if reverse else 0)
            ref = cum[mid:mid + 1]
            blk = cum[i * c:(i + 1) * c]
            if c >= 2 * SUBLANES:
                first_half, second_half = blk[:c // 2], blk[c // 2:]
                pieces += ([first_half - ref, ref - second_half] if reverse
                           else [ref - first_half, second_half - ref])
            else:
                pieces.append(-jnp.abs(blk - ref))
        return jnp.concatenate(pieces, axis=0)

    last_row = 0 if reverse else L - 1
    exponents = ([level_exponent(L >> lev) for lev in range(HG_MATMUL_LEVEL)]
                 + [x[(1 + i) * L:(2 + i) * L] for i in range(HG_LEVELS - HG_MATMUL_LEVEL)]
                 + [cum, cum[last_row:last_row + 1] - cum])
    decay = [jnp.exp2(e) for e in exponents]
    decay16 = [d.astype(BF16) for d in decay]
    dec16 = lambda blk, j: decay16[blk][:, j * dk:(j + 1) * dk]
    q16, key16 = q.astype(BF16), key.astype(BF16)
    qs = [chunk(q16, j) for j in range(G)]
    ks = [chunk(key16, j) for j in range(G)]
    vs = [chunk(v, j) for j in range(G)]
    qk = jnp.sum(q * key, axis=-1, keepdims=True)

    zero = jnp.zeros((L, dk), BF16)
    block_diag = lambda a, b: jnp.concatenate([jnp.concatenate([a, zero], axis=1),
                                               jnp.concatenate([zero, b], axis=1)], axis=0)
    row = lax.broadcasted_iota(jnp.int32, (L, 2 * L), 0)
    col = lax.broadcasted_iota(jnp.int32, (L, 2 * L), 1)
    pairs = range(0, G, 2)
    a = [jnp.where(col == row, chunk(qk, j), jnp.where(col == row + L, chunk(qk, j + 1), 0.0)) for j in pairs]
    for lev in range(HG_LEVELS):
        for i, j in enumerate(pairs):
            e0, e1 = dec16(lev, j), dec16(lev, j + 1)
            q_pair = jnp.concatenate([qs[j] * e0, qs[j + 1] * e1], axis=1)
            a[i] = a[i] + m_ref[lev] * _dot_nt(q_pair, block_diag(ks[j] * e0, ks[j + 1] * e1))
    o_local = []
    for i, j in enumerate(pairs):
        o_pair = _dot(a[i].astype(BF16), block_diag(vs[j], vs[j + 1]))
        o_local += [o_pair[:, :dk], o_pair[:, dk:]]
    q_in = [qs[j] * dec16(HG_LEVELS, j) for j in range(G)]
    st_add = [_dot_tn(vs[j], ks[j] * dec16(HG_LEVELS + 1, j)) for j in range(G)]
    e_all = [decay[HG_LEVELS][last_row:last_row + 1, j * dk:(j + 1) * dk] for j in range(G)]

    st = st_ref[...]
    for j in (reversed(range(G)) if reverse else range(G)):
        emit(pl.ds(base + j * L, L), o_local[j] + _dot_nt(q_in[j], st.astype(BF16)))
        st = st * e_all[j] + st_add[j]
    st_ref[...] = st


def _hgrn_kernel(q_ref, v_ref, zf_ref, zb_ref, g_ref, qc_ref, vc_ref, zfc_ref, zbc_ref, gc_ref,
                 lb_ref, ng_ref, pf_ref, pb_ref, mf_ref, mb_ref, *rest, ctx_out):
    if ctx_out:
        o_ref, oc_ref, st_ref, of_ref, ofc_ref = rest
    else:
        o_ref, st_ref, of_ref = rest
        oc_ref = ofc_ref = None
    norm_g = ng_ref[...]

    def readout(o, g):
        o = o * lax.rsqrt(jnp.mean(o * o, axis=-1, keepdims=True) + RMS_EPS) * norm_g
        return (o * _silu(g)).astype(BF16)

    def store_to(dst_ref):
        def emit(rows, o):
            dst_ref[rows, :] = o
        return emit

    def readout_to(dst_ref, fwd_ref, gate_ref):
        def emit(rows, o):
            dst_ref[rows, :] = readout(fwd_ref[rows, :] + o, gate_ref[rows, :])
        return emit

    def drop(rows, o):
        pass

    def scan(refs, lbp, p_ref, m_ref, reverse, emit):
        n_chunks = refs[0].shape[0] // HG_CHUNK
        group = min(HG_GROUP, n_chunks)
        n_groups = n_chunks // group

        def body(i, carry):
            first_chunk = ((n_groups - 1 - i) if reverse else i) * group
            _hgrn_group(*refs, first_chunk, group, lbp, p_ref, m_ref, st_ref, reverse, emit)
            return carry
        lax.fori_loop(0, n_groups, body, 0)

    lbf = (lb_ref[0, 0], lb_ref[0, 1], lb_ref[0, 2])
    st_ref[...] = jnp.zeros_like(st_ref)
    scan((qc_ref, vc_ref, zfc_ref), lbf, pf_ref, mf_ref, False, store_to(ofc_ref) if ctx_out else drop)
    scan((q_ref, v_ref, zf_ref), lbf, pf_ref, mf_ref, False, store_to(of_ref))

    lbb = (lb_ref[1, 0], lb_ref[1, 1], lb_ref[1, 2])
    st_ref[...] = jnp.zeros_like(st_ref)
    scan((qc_ref, vc_ref, zbc_ref), lbb, pb_ref, mb_ref, True,
         readout_to(oc_ref, ofc_ref, gc_ref) if ctx_out else drop)
    scan((q_ref, v_ref, zb_ref), lbb, pb_ref, mb_ref, True, readout_to(o_ref, of_ref, g_ref))


def _hgrn2(main, ctx, lb, norm_g, ctx_out):
    q = main[0]
    bsz, t, _ = q.shape
    lc = ctx[0].shape[1]
    pf, pb, mf, mb = _hgrn_tables()
    lb = lb.astype(F32)
    lb_params = jnp.stack([jnp.log(lb), jnp.log1p(-lb), 1.0 - lb], axis=1)
    lb_params = jnp.transpose(lb_params, (2, 0, 1, 3))[:, :, :, None, :]
    seq_spec = pl.BlockSpec((None, t, HG_HEAD_DIM), lambda b, h: (b, 0, h))
    ctx_spec = pl.BlockSpec((None, lc, HG_HEAD_DIM), lambda b, h: (b, 0, h))
    in_specs = ([seq_spec] * 5 + [ctx_spec] * 5 + [
        pl.BlockSpec((None, 2, 3, 1, HG_HEAD_DIM), lambda b, h: (h, 0, 0, 0, 0)),
        pl.BlockSpec((None, 1, HG_HEAD_DIM), lambda b, h: (h, 0, 0)),
        _const_spec(pf.shape), _const_spec(pb.shape), _const_spec(mf.shape), _const_spec(mb.shape)])
    out_specs = [seq_spec]
    out_shape = [jax.ShapeDtypeStruct((bsz, t, HG_DIM), BF16)]
    scratch = [pltpu.VMEM((HG_HEAD_DIM, HG_HEAD_DIM), F32), pltpu.VMEM((t, HG_HEAD_DIM), F32)]
    if ctx_out:
        out_specs.append(ctx_spec)
        out_shape.append(jax.ShapeDtypeStruct((bsz, lc, HG_DIM), BF16))
        scratch.append(pltpu.VMEM((lc, HG_HEAD_DIM), F32))
    outs = pl.pallas_call(
        functools.partial(_hgrn_kernel, ctx_out=ctx_out),
        grid=(bsz, HG_HEADS),
        in_specs=in_specs,
        out_specs=out_specs,
        out_shape=out_shape,
        scratch_shapes=scratch,
        compiler_params=_params(2),
        name="hgrn2",
    )(*main, *ctx, lb_params, norm_g.astype(F32)[:, None, :],
      jnp.asarray(pf, BF16), jnp.asarray(pb, BF16), jnp.asarray(mf), jnp.asarray(mb))
    return (outs[0], outs[1]) if ctx_out else (outs[0], None)


LANE_BLOCK = 128


def _conv_kernel(prev_ref, cur_ref, next_ref, w_ref, b_ref, g_ref, beta_ref, o_ref, win_ref, acc_ref,
                 shift_ref):
    i = pl.program_id(1)
    tm = cur_ref.shape[0]
    win_ref[0:CONV_HALO, :] = jnp.where(i > 0, prev_ref[...], 0.0)
    win_ref[CONV_HALO:CONV_HALO + tm, :] = cur_ref[...]
    win_ref[CONV_HALO + tm:, :] = jnp.where(i < pl.num_programs(1) - 1, next_ref[...], 0.0)
    first = CONV_HALO - CONV_WIDTH // 2
    span = tm + 2 * CONV_HALO - SUBLANES
    def lane_block(c, carry):
        cols = pl.ds(pl.multiple_of(c * LANE_BLOCK, LANE_BLOCK), LANE_BLOCK)
        for phase in range(SUBLANES):
            shift_ref[phase] = win_ref[phase:phase + span, cols]
        acc = jnp.zeros((tm, LANE_BLOCK), F32)
        for k in range(CONV_WIDTH):
            phase = (first + k) % SUBLANES
            off = first + k - phase
            acc = acc + shift_ref[phase, off:off + tm, :] * w_ref[k, :, cols]
        acc_ref[:, cols] = acc + b_ref[:, cols]
        return carry

    lax.fori_loop(0, D_MODEL // LANE_BLOCK, lane_block, 0)
    y = _layer_norm(acc_ref[...], g_ref[...], beta_ref[...])
    o_ref[...] = _silu(y).astype(o_ref.dtype)


def _conv_module_mid(u, w_dw, b_dw, ln_g, ln_b, tm):
    bsz, t, d = u.shape
    hb = tm // CONV_HALO
    n_hb = t // CONV_HALO
    vec = lambda a: a.astype(F32).reshape(1, d)
    w_taps = w_dw.astype(F32).reshape(CONV_WIDTH, 1, d)
    return pl.pallas_call(
        _conv_kernel,
        grid=(bsz, t // tm),
        in_specs=[
            pl.BlockSpec((None, CONV_HALO, d), lambda b, i: (b, jnp.maximum(i * hb - 1, 0), 0)),
            pl.BlockSpec((None, tm, d), lambda b, i: (b, i, 0)),
            pl.BlockSpec((None, CONV_HALO, d), lambda b, i: (b, jnp.minimum((i + 1) * hb, n_hb - 1), 0)),
            _const_spec((CONV_WIDTH, 1, d)), _const_spec((1, d)), _const_spec((1, d)), _const_spec((1, d)),
        ],
        out_specs=pl.BlockSpec((None, tm, d), lambda b, i: (b, i, 0)),
        out_shape=jax.ShapeDtypeStruct((bsz, t, d), BF16),
        scratch_shapes=[pltpu.VMEM((tm + 2 * CONV_HALO, d), F32), pltpu.VMEM((tm, d), F32),
                        pltpu.VMEM((SUBLANES, tm + 2 * CONV_HALO - SUBLANES, LANE_BLOCK), F32)],
        compiler_params=_params(2),
        name="conv_module",
    )(u, u, u, w_taps, vec(b_dw), vec(ln_g), vec(ln_b))


def _ffn_kernel(*refs, n_act):
    x_ref = refs[0]
    act_refs = refs[1:1 + n_act]
    wo_refs = refs[1 + n_act:1 + 2 * n_act]
    (bo_ref, g1_ref, sh2_ref, sc2_ref, g2_ref, l1g_ref, l1b_ref, l2g_ref, l2b_ref,
     w1_ref, w3_ref, w2_ref, o_ref) = refs[1 + 2 * n_act:]
    y = bo_ref[...]
    for a_ref, w_ref in zip(act_refs, wo_refs):
        y = y + _dot(a_ref[...], w_ref[...])
    x1 = _layer_norm(ALPHA * x_ref[...] + g1_ref[...] * y, l1g_ref[...], l1b_ref[...])
    h = (x1 * (1.0 + sc2_ref[...]) + sh2_ref[...]).astype(BF16)
    u = (_silu(_dot(h, w1_ref[...])) * _dot(h, w3_ref[...])).astype(BF16)
    f = _dot(u, w2_ref[...])
    o_ref[...] = _layer_norm(ALPHA * x1 + g2_ref[...] * f, l2g_ref[...], l2b_ref[...])


def _out_ffn(x, acts, w_outs, b_out, mods, ln, w1, w3, w2, tm):
    bsz, t, d = x.shape
    n_act = len(acts)
    tok = lambda width: pl.BlockSpec((None, tm, width), lambda b, i: (b, i, 0))
    vec = lambda a: a.astype(F32).reshape(1, d)
    g1, sh2, sc2, g2 = mods
    in_specs = ([tok(d)] + [tok(a.shape[-1]) for a in acts] + [_const_spec(w.shape) for w in w_outs]
                + [_const_spec((1, d))] + [_mod_spec(m) for m in mods] + [_const_spec((1, d))] * 4
                + [_const_spec(w1.shape), _const_spec(w3.shape), _const_spec(w2.shape)])
    return pl.pallas_call(
        functools.partial(_ffn_kernel, n_act=n_act),
        grid=(bsz, t // tm),
        in_specs=in_specs,
        out_specs=tok(d),
        out_shape=jax.ShapeDtypeStruct((bsz, t, d), F32),
        compiler_params=_params(2),
        name="out_ffn",
    )(x, *acts, *w_outs, vec(b_out), g1, sh2, sc2, g2, *[vec(a) for a in ln], w1, w3, w2)


def kernel(x, c, ctx, c_ctx, w_mod, b_mod, ln1_g, ln1_b, ln2_g, ln2_b, w_in_even, w_out_even, na_rpb, hg_lb,
           hg_norm_g, cv_w1, cv_b1, cv_wdw, cv_bdw, cv_ln_g, cv_ln_b, cv_w2, cv_b2, ffn_w1, ffn_w3, ffn_w2):
    bsz, t, d = x.shape
    lc = ctx.shape[1]
    depth = w_mod.shape[0]
    last_even = depth - 1 if (depth - 1) % 2 == 0 else depth - 2
    tm = min(512, t)
    tmc = min(512, lc)

    p_lb = jax.nn.softmax(hg_lb.astype(F32), axis=1)
    hg_lower = jnp.maximum(jnp.cumsum(p_lb, axis=1) - p_lb[:, :1], 0.0)

    c_all = jnp.zeros((MOD_ROWS, d), F32).at[:bsz].set(c).at[bsz].set(c_ctx)
    mod_all = _modulation(c_all, w_mod, b_mod)

    xc = ctx
    for l in range(depth):
        ctx_used = l <= last_even
        ctx_out = l < last_even
        mod = [mod_all[l, :bsz, k * d:(k + 1) * d].reshape(bsz, 1, d) for k in range(6)]
        mod_c = [mod_all[l, bsz:bsz + 1, k * d:(k + 1) * d].reshape(1, 1, d) for k in range(6)]
        ln = (ln1_g[l], ln1_b[l], ln2_g[l], ln2_b[l])
        w1, w3, w2 = ffn_w1[l].astype(BF16), ffn_w3[l].astype(BF16), ffn_w2[l].astype(BF16)
        j = l // 2
        if l % 2 == 0:
            w_in = w_in_even[j].astype(BF16)
            w_out = w_out_even[j].astype(BF16)
            w_outs = [w_out[:NA_DIM], w_out[NA_DIM:]]
            b_out = jnp.zeros((d,), F32)
            qa, ka, va, qb, ib, zf, zb, gb = _inproj_even(x, mod[0], mod[1], w_in, tm)
            qa_c, ka_c, va_c, qb_c, ib_c, zf_c, zb_c, gb_c = _inproj_even(xc, mod_c[0], mod_c[1], w_in, tmc)
            o_na = _neighbourhood_attention(qa, ka, va, ka_c, va_c, na_rpb[j])
            o_hg, o_hg_c = _hgrn2((qb, ib, zf, zb, gb), (qb_c, ib_c, zf_c, zb_c, gb_c), hg_lower[:, j],
                                  hg_norm_g[j], ctx_out)
            acts = [o_na, o_hg]
            if ctx_out:
                acts_c = [_context_attention(qa_c, ka_c, va_c), o_hg_c]
        else:
            w_outs = [cv_w2[j].astype(BF16)]
            b_out = cv_b2[j]
            conv = (cv_wdw[j], cv_bdw[j], cv_ln_g[j], cv_ln_b[j])
            w_cv1 = cv_w1[j].astype(BF16)
            b_cv1 = cv_b1[j].astype(F32).reshape(1, 2 * d)
            acts = [_conv_module_mid(_inproj_odd(x, mod[0], mod[1], w_cv1, b_cv1, tm), *conv, min(256, t))]
            if ctx_out:
                acts_c = [_conv_module_mid(_inproj_odd(xc, mod_c[0], mod_c[1], w_cv1, b_cv1, tmc), *conv,
                                           min(256, lc))]
        x = _out_ffn(x, acts, w_outs, b_out, mod[2:], ln, w1, w3, w2, tm)
        if ctx_out:
            xc = _out_ffn(xc, acts_c, w_outs, b_out, mod_c[2:], ln, w1, w3, w2, tmc)
    return x
```

```python
import functools

import numpy as np
import jax
import jax.numpy as jnp
from jax import lax
from jax.experimental import pallas as pl
from jax.experimental.pallas import tpu as pltpu

F32 = jnp.float32
BF16 = jnp.bfloat16

D_MODEL = 1024
DEPTH = 4
GRID_W = 64
NA_HEADS = 8
NA_HEAD_DIM = 64
NA_DIM = NA_HEADS * NA_HEAD_DIM
NA_WIN_H = 8
NA_WIN_W = 16
HG_HEADS = 4
HG_HEAD_DIM = 128
HG_DIM = HG_HEADS * HG_HEAD_DIM
HG_CHUNK = 64
HG_LEVELS = 6
HG_GROUP = 32
SUBLANES = 8
HG_MATMUL_LEVEL = 4
LOG2_E = 1.4426950408889634
CONV_WIDTH = 31
CONV_HALO = 16
D_FF = 2816
ALPHA = (2 * DEPTH) ** 0.25
LN_EPS = 1e-5
FFN_ROW_SPLIT = 2
RMS_EPS = 1e-6

VMEM_LIMIT_BYTES = 56 * 1024 * 1024
MOD_ROWS = 24


def _params(n_grid, vmem=VMEM_LIMIT_BYTES):
    return pltpu.CompilerParams(dimension_semantics=("arbitrary",) * n_grid, vmem_limit_bytes=vmem)


def _const_spec(shape):
    nd = len(shape)
    return pl.BlockSpec(shape, lambda *_: (0,) * nd, pipeline_mode=pl.Buffered(1))


def _layer_norm(x, g, b):
    mu = jnp.mean(x, axis=-1, keepdims=True)
    xc = x - mu
    var = jnp.mean(xc * xc, axis=-1, keepdims=True)
    return xc * lax.rsqrt(var + LN_EPS) * g + b


def _silu(x):
    return x * jax.nn.sigmoid(x)


def _dot(a, b):
    return jnp.dot(a, b, preferred_element_type=F32)


def _dot_nt(a, b):
    return lax.dot_general(a, b, (((1,), (1,)), ((), ())), preferred_element_type=F32)


def _dot_tn(a, b):
    return lax.dot_general(a, b, (((0,), (0,)), ((), ())), preferred_element_type=F32)


def _mod_kernel(c_ref, w_ref, b_ref, o_ref):
    s = _silu(c_ref[...]).astype(BF16)
    o_ref[...] = _dot(s, w_ref[...].astype(BF16)) + b_ref[...]


def _modulation(c_all, w_mod, b_mod):
    depth, d, n = w_mod.shape
    tn = 1536
    return pl.pallas_call(
        _mod_kernel,
        grid=(depth, n // tn),
        in_specs=[
            pl.BlockSpec((MOD_ROWS, d), lambda l, j: (0, 0)),
            pl.BlockSpec((None, d, tn), lambda l, j: (l, 0, j)),
            pl.BlockSpec((None, 1, tn), lambda l, j: (l, 0, j)),
        ],
        out_specs=pl.BlockSpec((None, MOD_ROWS, tn), lambda l, j: (l, 0, j)),
        out_shape=jax.ShapeDtypeStruct((depth, MOD_ROWS, n), F32),
        compiler_params=_params(2),
        name="modulation",
    )(c_all, w_mod, b_mod.reshape(depth, 1, n))


def _mod_spec(mod):
    per_batch = 1 if mod.shape[0] > 1 else 0
    return pl.BlockSpec((None, 1, D_MODEL), lambda b, i: (b * per_batch, 0, 0))


def _inproj_even_kernel(x_ref, sh_ref, sc_ref, w_ref, qa_ref, ka_ref, va_ref, qb_ref, ib_ref, zf_ref,
                        zb_ref, gb_ref):
    h = (x_ref[...] * (1.0 + sc_ref[...]) + sh_ref[...]).astype(BF16)
    outs = (qa_ref, ka_ref, va_ref, qb_ref, ib_ref, zf_ref, zb_ref, gb_ref)
    for s, o_ref in enumerate(outs):
        y = _dot(h, w_ref[:, s * NA_DIM:(s + 1) * NA_DIM])
        if s == 0:
            y = y * (NA_HEAD_DIM ** -0.5 * LOG2_E)
        o_ref[...] = y.astype(o_ref.dtype)


def _inproj_even(x, sh, sc, w_bf16, tm):
    bsz, t, d = x.shape
    tok = pl.BlockSpec((None, tm, d), lambda b, i: (b, i, 0))
    out_spec = pl.BlockSpec((None, tm, NA_DIM), lambda b, i: (b, i, 0))
    dtypes = (BF16, BF16, BF16, F32, BF16, F32, F32, F32)
    return pl.pallas_call(
        _inproj_even_kernel,
        grid=(bsz, t // tm),
        in_specs=[tok, _mod_spec(sh), _mod_spec(sc), _const_spec(w_bf16.shape)],
        out_specs=[out_spec] * 8,
        out_shape=[jax.ShapeDtypeStruct((bsz, t, NA_DIM), dt) for dt in dtypes],
        compiler_params=_params(2),
        name="inproj_even",
    )(x, sh, sc, w_bf16)


def _inproj_odd_kernel(x_ref, sh_ref, sc_ref, w_ref, b_ref, u_ref):
    h = (x_ref[...] * (1.0 + sc_ref[...]) + sh_ref[...]).astype(BF16)
    a = _dot(h, w_ref[:, :D_MODEL]) + b_ref[:, :D_MODEL]
    g = _dot(h, w_ref[:, D_MODEL:]) + b_ref[:, D_MODEL:]
    u_ref[...] = a * jax.nn.sigmoid(g)


def _inproj_odd(x, sh, sc, w_bf16, bias, tm):
    bsz, t, d = x.shape
    tok = pl.BlockSpec((None, tm, d), lambda b, i: (b, i, 0))
    return pl.pallas_call(
        _inproj_odd_kernel,
        grid=(bsz, t // tm),
        in_specs=[tok, _mod_spec(sh), _mod_spec(sc), _const_spec(w_bf16.shape), _const_spec(bias.shape)],
        out_specs=tok,
        out_shape=jax.ShapeDtypeStruct((bsz, t, d), F32),
        compiler_params=_params(2),
        name="inproj_odd",
    )(x, sh, sc, w_bf16, bias)


HEADS_PER_GROUP = 4
GROUP_W = HEADS_PER_GROUP * NA_HEAD_DIM
NA_TILE_ROWS = 4


def _softmax_pv(s_parts, v_parts):
    m = s_parts[0].max(axis=-1, keepdims=True)
    for s in s_parts[1:]:
        m = jnp.maximum(m, s.max(axis=-1, keepdims=True))
    ps = [jnp.exp2(s - m) for s in s_parts]
    l = ps[0].sum(axis=-1, keepdims=True)
    for p in ps[1:]:
        l = l + p.sum(axis=-1, keepdims=True)
    o = _dot(ps[0].astype(BF16), v_parts[0])
    for p, v in zip(ps[1:], v_parts[1:]):
        o = o + _dot(p.astype(BF16), v)
    return o * (1.0 / l)


def _na_kernel(q_ref, k_ref, v_ref, kc_ref, vc_ref, bias_ref, o_ref, *, rows, key_rows):
    r0 = pl.program_id(1) * NA_TILE_ROWS
    first_row = jnp.clip(r0 - NA_WIN_H // 2, 0, rows - key_rows)
    start = pl.multiple_of(first_row * GRID_W, GRID_W)
    nq = q_ref.shape[0]
    nk = key_rows * GRID_W
    lane = lax.broadcasted_iota(jnp.int32, (nq, GROUP_W), 1)
    head_lanes = [(lane >= hh * NA_HEAD_DIM) & (lane < (hh + 1) * NA_HEAD_DIM) for hh in range(HEADS_PER_GROUP)]
    for g in range(NA_HEADS // HEADS_PER_GROUP):
        cols = slice(g * GROUP_W, (g + 1) * GROUP_W)
        qg = q_ref[:, cols]
        kg = k_ref[pl.ds(start, nk), cols]
        vg = v_ref[pl.ds(start, nk), cols]
        kcg = kc_ref[:, cols]
        vcg = vc_ref[:, cols]
        scores = []
        for hh in range(HEADS_PER_GROUP):
            qh = jnp.where(head_lanes[hh], qg, jnp.zeros_like(qg))
            scores.append([_dot_nt(qh, kg) + bias_ref[g * HEADS_PER_GROUP + hh], _dot_nt(qh, kcg)])
        acc = jnp.zeros((nq, GROUP_W), F32)
        for hh in range(HEADS_PER_GROUP):
            acc = jnp.where(head_lanes[hh], _softmax_pv(scores[hh], [vg, vcg]), acc)
        o_ref[:, cols] = acc.astype(o_ref.dtype)


def _na_tile_geometry(rows):
    kr = min(NA_WIN_H, rows)
    key_rows = min(NA_TILE_ROWS + NA_WIN_H, rows)
    outside = 2 * NA_WIN_H - 1
    geo = []
    for r0 in range(0, rows, NA_TILE_ROWS):
        first = int(np.clip(r0 - NA_WIN_H // 2, 0, rows - key_rows))
        idx = np.full((NA_TILE_ROWS, key_rows), outside, np.int32)
        for i in range(NA_TILE_ROWS):
            r = r0 + i
            rs = int(np.clip(r - kr // 2, 0, rows - kr))
            assert first <= rs and rs + kr <= first + key_rows
            for j in range(rs - first, rs - first + kr):
                idx[i, j] = first + j - r + NA_WIN_H - 1
        geo.append(idx)
    return key_rows, geo


def _na_bias_table(rpb, rows):
    key_rows, geo = _na_tile_geometry(rows)
    assert all(np.array_equal(g, geo[1]) for g in geo[1:-1])
    col = np.arange(GRID_W)
    col_start = np.clip(col - NA_WIN_W // 2, 0, GRID_W - NA_WIN_W)
    col_in = (col[None, :] >= col_start[:, None]) & (col[None, :] < col_start[:, None] + NA_WIN_W)
    dc_idx = np.clip(col[None, :] - col[:, None] + NA_WIN_W - 1, 0, 2 * NA_WIN_W - 2)
    per_dr = jnp.where(col_in[None, None], rpb[:, :, dc_idx].astype(F32) * LOG2_E, -jnp.inf)
    per_dr = jnp.concatenate([per_dr, jnp.full_like(per_dr[:, :1], -jnp.inf)], axis=1)
    tabs = []
    for idx in (geo[0], geo[1], geo[-1]):
        blocks = per_dr[:, idx]
        tabs.append(jnp.transpose(blocks, (0, 1, 3, 2, 4)).reshape(
            NA_HEADS, NA_TILE_ROWS * GRID_W, key_rows * GRID_W))
    return key_rows, jnp.stack(tabs)


def _neighbourhood_attention(q, k, v, kc, vc, rpb):
    bsz, t, _ = q.shape
    lc = kc.shape[1]
    rows = t // GRID_W
    n_tiles = rows // NA_TILE_ROWS
    assert n_tiles >= 3
    key_rows, bias = _na_bias_table(rpb, rows)
    nq = NA_TILE_ROWS * GRID_W

    def bias_index(b, i):
        return (jnp.minimum(i, 1) + (i == n_tiles - 1).astype(jnp.int32), 0, 0, 0)

    tile_spec = pl.BlockSpec((None, nq, NA_DIM), lambda b, i: (b, i, 0))
    seq_spec = pl.BlockSpec((None, t, NA_DIM), lambda b, i: (b, 0, 0))
    ctx_spec = pl.BlockSpec((None, lc, NA_DIM), lambda b, i: (b, 0, 0))
    return pl.pallas_call(
        functools.partial(_na_kernel, rows=rows, key_rows=key_rows),
        grid=(bsz, n_tiles),
        in_specs=[tile_spec, seq_spec, seq_spec, ctx_spec, ctx_spec,
                  pl.BlockSpec((None, NA_HEADS, nq, key_rows * GRID_W), bias_index)],
        out_specs=tile_spec,
        out_shape=jax.ShapeDtypeStruct((bsz, t, NA_DIM), BF16),
        compiler_params=_params(2),
        name="neighbourhood_attention",
    )(q, k, v, kc, vc, bias)


def _ctx_attn_kernel(q_ref, k_ref, v_ref, o_ref):
    nq = q_ref.shape[0]
    lane = lax.broadcasted_iota(jnp.int32, (nq, GROUP_W), 1)
    for g in range(NA_HEADS // HEADS_PER_GROUP):
        cols = slice(g * GROUP_W, (g + 1) * GROUP_W)
        qg = q_ref[:, cols]
        kg = k_ref[:, cols]
        vg = v_ref[:, cols]
        acc = jnp.zeros((nq, GROUP_W), F32)
        for hh in range(HEADS_PER_GROUP):
            head_lanes = (lane >= hh * NA_HEAD_DIM) & (lane < (hh + 1) * NA_HEAD_DIM)
            qh = jnp.where(head_lanes, qg, jnp.zeros_like(qg))
            o = _softmax_pv([_dot_nt(qh, kg)], [vg])
            acc = jnp.where(head_lanes, o, acc)
        o_ref[:, cols] = acc.astype(o_ref.dtype)


def _context_attention(q, k, v):
    bsz, lc, _ = q.shape
    spec = pl.BlockSpec((None, lc, NA_DIM), lambda b: (b, 0, 0))
    return pl.pallas_call(
        _ctx_attn_kernel,
        grid=(bsz,),
        in_specs=[spec, spec, spec],
        out_specs=spec,
        out_shape=jax.ShapeDtypeStruct((bsz, lc, NA_DIM), BF16),
        compiler_params=_params(1),
        name="context_attention",
    )(q, k, v)


def _hgrn_tables():
    L = HG_CHUNK
    t = np.arange(L)
    p_rows, masks = [], []
    for lev in range(HG_LEVELS):
        c = L >> lev
        mid = (t // c) * c + c // 2
        P = np.zeros((L, L), np.float32)
        for i in range(L):
            m = mid[i]
            if i >= m:
                P[i, m + 1:i + 1] = 1.0
            else:
                P[i, i + 1:m + 1] = 1.0
        p_rows.append(P)
        same = (t[:, None] // c) == (t[None, :] // c)
        masks.append((same & (t[:, None] >= mid[:, None]) & (t[None, :] < mid[None, :])).astype(np.float32))
    incl = np.tril(np.ones((L, L), np.float32))
    fine = p_rows[HG_MATMUL_LEVEL:]
    pf = np.concatenate([incl] + fine, axis=0)
    flip = lambda a: a[..., ::-1, ::-1]
    pb = np.concatenate([flip(incl)] + [flip(p) for p in fine], axis=0)
    side_by_side = lambda ms: np.stack([np.concatenate([m, m], axis=1) for m in ms])
    mf = side_by_side(masks)
    mb = side_by_side([flip(m) for m in masks])
    return pf, pb, mf, mb


def _hgrn_gates(z, log_lb, log1m_lb, one_m_lb):
    e = jnp.exp(-jnp.abs(z))
    inv = 1.0 / (1.0 + e)
    key = one_m_lb * jnp.where(z >= 0, e * inv, inv)
    c = log1m_lb + (jnp.minimum(z, 0.0) - jnp.log1p(e))
    log_f = jnp.maximum(log_lb, c) + jnp.log1p(jnp.exp(-jnp.abs(log_lb - c)))
    return key, log_f


def _hgrn_group(q_ref, v_ref, z_ref, first_chunk, G, lbp, p_ref, m_ref, st_ref, reverse, emit):
    L, dk = HG_CHUNK, HG_HEAD_DIM
    base = pl.multiple_of(first_chunk * L, G * L)
    group_rows = pl.ds(base, G * L)
    chunk = lambda a, j: a[j * L:(j + 1) * L]
    wide = lambda a: jnp.concatenate([chunk(a, j) for j in range(G)], axis=1)

    q, v, z = q_ref[group_rows, :], v_ref[group_rows, :], z_ref[group_rows, :]
    key, log_f = _hgrn_gates(z, lbp[0], lbp[1], lbp[2])
    log2_f = log_f * LOG2_E
    g_hi = log2_f.astype(BF16)
    g_lo = (log2_f - g_hi.astype(F32)).astype(BF16)
    p = p_ref[...]
    x = _dot(p, wide(g_hi)) + _dot(p, wide(g_lo))
    cum = x[:L]

    def level_exponent(c):
        pieces = []
        for i in range(L // c):
            mid = i * c + c // 2 - (1 if reverse else 0)
            ref = cum[mid:mid + 1]
            blk = cum[i * c:(i + 1) * c]
            if c >= 2 * SUBLANES:
                first_half, second_half = blk[:c // 2], blk[c // 2:]
                pieces += ([first_half - ref, ref - second_half] if reverse
                           else [ref - first_half, second_half - ref])
            else:
                pieces.append(-jnp.abs(blk - ref))
        return jnp.concatenate(pieces, axis=0)

    last_row = 0 if reverse else L - 1
    exponents = ([level_exponent(L >> lev) for lev in range(HG_MATMUL_LEVEL)]
                 + [x[(1 + i) * L:(2 + i) * L] for i in range(HG_LEVELS - HG_MATMUL_LEVEL)]
                 + [cum, cum[last_row:last_row + 1] - cum])
    decay = [jnp.exp2(e) for e in exponents]
    decay16 = [d.astype(BF16) for d in decay]
    dec16 = lambda blk, j: decay16[blk][:, j * dk:(j + 1) * dk]
    q16, key16 = q.astype(BF16), key.astype(BF16)
    qs = [chunk(q16, j) for j in range(G)]
    ks = [chunk(key16, j) for j in range(G)]
    vs = [chunk(v, j) for j in range(G)]
    qk = jnp.sum(q * key, axis=-1, keepdims=True)

    zero = jnp.zeros((L, dk), BF16)
    block_diag = lambda a, b: jnp.concatenate([jnp.concatenate([a, zero], axis=1),
                                               jnp.concatenate([zero, b], axis=1)], axis=0)
    row = lax.broadcasted_iota(jnp.int32, (L, 2 * L), 0)
    col = lax.broadcasted_iota(jnp.int32, (L, 2 * L), 1)
    pairs = range(0, G, 2)
    a = [jnp.where(col == row, chunk(qk, j), jnp.where(col == row + L, chunk(qk, j + 1), 0.0)) for j in pairs]
    for lev in range(HG_LEVELS):
        for i, j in enumerate(pairs):
            e0, e1 = dec16(lev, j), dec16(lev, j + 1)
            q_pair = jnp.concatenate([qs[j] * e0, qs[j + 1] * e1], axis=1)
            a[i] = a[i] + m_ref[lev] * _dot_nt(q_pair, block_diag(ks[j] * e0, ks[j + 1] * e1))
    o_local = []
    for i, j in enumerate(pairs):
        o_pair = _dot(a[i].astype(BF16), block_diag(vs[j], vs[j + 1]))
        o_local += [o_pair[:, :dk], o_pair[:, dk:]]
    q_in = [qs[j] * dec16(HG_LEVELS, j) for j in range(G)]
    st_add = [_dot_tn(vs[j], ks[j] * dec16(HG_LEVELS + 1, j)) for j in range(G)]
    e_all = [decay[HG_LEVELS][last_row:last_row + 1, j * dk:(j + 1) * dk] for j in range(G)]

    st = st_ref[...]
    for j in (reversed(range(G)) if reverse else range(G)):
        emit(pl.ds(base + j * L, L), o_local[j] + _dot_nt(q_in[j], st.astype(BF16)))
        st = st * e_all[j] + st_add[j]
    st_ref[...] = st


def _hgrn_kernel(q_ref, v_ref, zf_ref, zb_ref, g_ref, qc_ref, vc_ref, zfc_ref, zbc_ref, gc_ref,
                 lb_ref, ng_ref, pf_ref, pb_ref, mf_ref, mb_ref, *rest, ctx_out):
    if ctx_out:
        o_ref, oc_ref, st_ref, of_ref, ofc_ref = rest
    else:
        o_ref, st_ref, of_ref = rest
        oc_ref = ofc_ref = None
    norm_g = ng_ref[...]

    def readout(o, g):
        o = o * lax.rsqrt(jnp.mean(o * o, axis=-1, keepdims=True) + RMS_EPS) * norm_g
        return (o * _silu(g)).astype(BF16)

    def store_to(dst_ref):
        def emit(rows, o):
            dst_ref[rows, :] = o
        return emit

    def readout_to(dst_ref, fwd_ref, gate_ref):
        def emit(rows, o):
            dst_ref[rows, :] = readout(fwd_ref[rows, :] + o, gate_ref[rows, :])
        return emit

    def drop(rows, o):
        pass

    def scan(refs, lbp, p_ref, m_ref, reverse, emit):
        n_chunks = refs[0].shape[0] // HG_CHUNK
        group = min(HG_GROUP, n_chunks)
        n_groups = n_chunks // group

        def body(i, carry):
            first_chunk = ((n_groups - 1 - i) if reverse else i) * group
            _hgrn_group(*refs, first_chunk, group, lbp, p_ref, m_ref, st_ref, reverse, emit)
            return carry
        lax.fori_loop(0, n_groups, body, 0)

    lbf = (lb_ref[0, 0], lb_ref[0, 1], lb_ref[0, 2])
    st_ref[...] = jnp.zeros_like(st_ref)
    scan((qc_ref, vc_ref, zfc_ref), lbf, pf_ref, mf_ref, False, store_to(ofc_ref) if ctx_out else drop)
    scan((q_ref, v_ref, zf_ref), lbf, pf_ref, mf_ref, False, store_to(of_ref))

    lbb = (lb_ref[1, 0], lb_ref[1, 1], lb_ref[1, 2])
    st_ref[...] = jnp.zeros_like(st_ref)
    scan((qc_ref, vc_ref, zbc_ref), lbb, pb_ref, mb_ref, True,
         readout_to(oc_ref, ofc_ref, gc_ref) if ctx_out else drop)
    scan((q_ref, v_ref, zb_ref), lbb, pb_ref, mb_ref, True, readout_to(o_ref, of_ref, g_ref))


def _hgrn2(main, ctx, lb, norm_g, ctx_out):
    q = main[0]
    bsz, t, _ = q.shape
    lc = ctx[0].shape[1]
    pf, pb, mf, mb = _hgrn_tables()
    lb = lb.astype(F32)
    lb_params = jnp.stack([jnp.log(lb), jnp.log1p(-lb), 1.0 - lb], axis=1)
    lb_params = jnp.transpose(lb_params, (2, 0, 1, 3))[:, :, :, None, :]
    seq_spec = pl.BlockSpec((None, t, HG_HEAD_DIM), lambda b, h: (b, 0, h))
    ctx_spec = pl.BlockSpec((None, lc, HG_HEAD_DIM), lambda b, h: (b, 0, h))
    in_specs = ([seq_spec] * 5 + [ctx_spec] * 5 + [
        pl.BlockSpec((None, 2, 3, 1, HG_HEAD_DIM), lambda b, h: (h, 0, 0, 0, 0)),
        pl.BlockSpec((None, 1, HG_HEAD_DIM), lambda b, h: (h, 0, 0)),
        _const_spec(pf.shape), _const_spec(pb.shape), _const_spec(mf.shape), _const_spec(mb.shape)])
    out_specs = [seq_spec]
    out_shape = [jax.ShapeDtypeStruct((bsz, t, HG_DIM), BF16)]
    scratch = [pltpu.VMEM((HG_HEAD_DIM, HG_HEAD_DIM), F32), pltpu.VMEM((t, HG_HEAD_DIM), F32)]
    if ctx_out:
        out_specs.append(ctx_spec)
        out_shape.append(jax.ShapeDtypeStruct((bsz, lc, HG_DIM), BF16))
        scratch.append(pltpu.VMEM((lc, HG_HEAD_DIM), F32))
    outs = pl.pallas_call(
        functools.partial(_hgrn_kernel, ctx_out=ctx_out),
        grid=(bsz, HG_HEADS),
        in_specs=in_specs,
        out_specs=out_specs,
        out_shape=out_shape,
        scratch_shapes=scratch,
        compiler_params=_params(2),
        name="hgrn2",
    )(*main, *ctx, lb_params, norm_g.astype(F32)[:, None, :],
      jnp.asarray(pf, BF16), jnp.asarray(pb, BF16), jnp.asarray(mf), jnp.asarray(mb))
    return (outs[0], outs[1]) if ctx_out else (outs[0], None)


LANE_BLOCK = 128


def _conv_kernel(prev_ref, cur_ref, next_ref, w_ref, b_ref, g_ref, beta_ref, o_ref, win_ref, acc_ref,
                 shift_ref):
    i = pl.program_id(1)
    tm = cur_ref.shape[0]
    win_ref[0:CONV_HALO, :] = jnp.where(i > 0, prev_ref[...], 0.0)
    win_ref[CONV_HALO:CONV_HALO + tm, :] = cur_ref[...]
    win_ref[CONV_HALO + tm:, :] = jnp.where(i < pl.num_programs(1) - 1, next_ref[...], 0.0)
    first = CONV_HALO - CONV_WIDTH // 2
    span = tm + 2 * CONV_HALO - SUBLANES
    def lane_block(c, carry):
        cols = pl.ds(pl.multiple_of(c * LANE_BLOCK, LANE_BLOCK), LANE_BLOCK)
        for phase in range(SUBLANES):
            shift_ref[phase] = win_ref[phase:phase + span, cols]
        acc = jnp.zeros((tm, LANE_BLOCK), F32)
        for k in range(CONV_WIDTH):
            phase = (first + k) % SUBLANES
            off = first + k - phase
            acc = acc + shift_ref[phase, off:off + tm, :] * w_ref[k, :, cols]
        acc_ref[:, cols] = acc + b_ref[:, cols]
        return carry

    lax.fori_loop(0, D_MODEL // LANE_BLOCK, lane_block, 0)
    y = _layer_norm(acc_ref[...], g_ref[...], beta_ref[...])
    o_ref[...] = _silu(y).astype(o_ref.dtype)


def _conv_module_mid(u, w_dw, b_dw, ln_g, ln_b, tm):
    bsz, t, d = u.shape
    hb = tm // CONV_HALO
    n_hb = t // CONV_HALO
    vec = lambda a: a.astype(F32).reshape(1, d)
    w_taps = w_dw.astype(F32).reshape(CONV_WIDTH, 1, d)
    return pl.pallas_call(
        _conv_kernel,
        grid=(bsz, t // tm),
        in_specs=[
            pl.BlockSpec((None, CONV_HALO, d), lambda b, i: (b, jnp.maximum(i * hb - 1, 0), 0)),
            pl.BlockSpec((None, tm, d), lambda b, i: (b, i, 0)),
            pl.BlockSpec((None, CONV_HALO, d), lambda b, i: (b, jnp.minimum((i + 1) * hb, n_hb - 1), 0)),
            _const_spec((CONV_WIDTH, 1, d)), _const_spec((1, d)), _const_spec((1, d)), _const_spec((1, d)),
        ],
        out_specs=pl.BlockSpec((None, tm, d), lambda b, i: (b, i, 0)),
        out_shape=jax.ShapeDtypeStruct((bsz, t, d), BF16),
        scratch_shapes=[pltpu.VMEM((tm + 2 * CONV_HALO, d), F32), pltpu.VMEM((tm, d), F32),
                        pltpu.VMEM((SUBLANES, tm + 2 * CONV_HALO - SUBLANES, LANE_BLOCK), F32)],
        compiler_params=_params(2),
        name="conv_module",
    )(u, u, u, w_taps, vec(b_dw), vec(ln_g), vec(ln_b))


def _ffn_kernel(*refs, n_act):
    x_ref = refs[0]
    act_refs = refs[1:1 + n_act]
    wo_refs = refs[1 + n_act:1 + 2 * n_act]
    (bo_ref, g1_ref, sh2_ref, sc2_ref, g2_ref, l1g_ref, l1b_ref, l2g_ref, l2b_ref,
     w1_ref, w3_ref, w2_ref, o_ref) = refs[1 + 2 * n_act:]
    sub = x_ref.shape[0] // FFN_ROW_SPLIT
    rows = [slice(r * sub, (r + 1) * sub) for r in range(FFN_ROW_SPLIT)]

    def out_proj(r):
        y = bo_ref[...]
        for a_ref, w_ref in zip(act_refs, wo_refs):
            y = y + _dot(a_ref[rows[r], :], w_ref[...])
        return y

    def norm1(r, y):
        x1 = _layer_norm(ALPHA * x_ref[rows[r], :] + g1_ref[...] * y, l1g_ref[...], l1b_ref[...])
        return x1, (x1 * (1.0 + sc2_ref[...]) + sh2_ref[...]).astype(BF16)

    def up(r, h):
        return _dot(h, w1_ref[...]), _dot(h, w3_ref[...])

    def gate(r, h1, h3):
        return (_silu(h1) * h3).astype(BF16)

    def down(r, u):
        return _dot(u, w2_ref[...])

    def norm2(r, x1, f):
        o_ref[rows[r], :] = _layer_norm(ALPHA * x1 + g2_ref[...] * f, l2g_ref[...], l2b_ref[...])

    stages = 6
    state = [None] * FFN_ROW_SPLIT
    for step in range(stages + FFN_ROW_SPLIT - 1):
        for r in range(FFN_ROW_SPLIT):
            stage = step - r
            if stage == 0:
                state[r] = out_proj(r)
            elif stage == 1:
                state[r] = norm1(r, state[r])
            elif stage == 2:
                x1, h = state[r]
                state[r] = (x1, up(r, h))
            elif stage == 3:
                x1, (h1, h3) = state[r]
                state[r] = (x1, gate(r, h1, h3))
            elif stage == 4:
                x1, u = state[r]
                state[r] = (x1, down(r, u))
            elif stage == 5:
                norm2(r, *state[r])


def _out_ffn(x, acts, w_outs, b_out, mods, ln, w1, w3, w2, tm):
    bsz, t, d = x.shape
    n_act = len(acts)
    tok = lambda width: pl.BlockSpec((None, tm, width), lambda b, i: (b, i, 0))
    vec = lambda a: a.astype(F32).reshape(1, d)
    g1, sh2, sc2, g2 = mods
    in_specs = ([tok(d)] + [tok(a.shape[-1]) for a in acts] + [_const_spec(w.shape) for w in w_outs]
                + [_const_spec((1, d))] + [_mod_spec(m) for m in mods] + [_const_spec((1, d))] * 4
                + [_const_spec(w1.shape), _const_spec(w3.shape), _const_spec(w2.shape)])
    return pl.pallas_call(
        functools.partial(_ffn_kernel, n_act=n_act),
        grid=(bsz, t // tm),
        in_specs=in_specs,
        out_specs=tok(d),
        out_shape=jax.ShapeDtypeStruct((bsz, t, d), F32),
        compiler_params=_params(2),
        name="out_ffn",
    )(x, *acts, *w_outs, vec(b_out), g1, sh2, sc2, g2, *[vec(a) for a in ln], w1, w3, w2)


def kernel(x, c, ctx, c_ctx, w_mod, b_mod, ln1_g, ln1_b, ln2_g, ln2_b, w_in_even, w_out_even, na_rpb, hg_lb,
           hg_norm_g, cv_w1, cv_b1, cv_wdw, cv_bdw, cv_ln_g, cv_ln_b, cv_w2, cv_b2, ffn_w1, ffn_w3, ffn_w2):
    bsz, t, d = x.shape
    lc = ctx.shape[1]
    depth = w_mod.shape[0]
    last_even = depth - 1 if (depth - 1) % 2 == 0 else depth - 2
    tm = min(512, t)
    tmc = min(512, lc)

    p_lb = jax.nn.softmax(hg_lb.astype(F32), axis=1)
    hg_lower = jnp.maximum(jnp.cumsum(p_lb, axis=1) - p_lb[:, :1], 0.0)

    c_all = jnp.zeros((MOD_ROWS, d), F32).at[:bsz].set(c).at[bsz].set(c_ctx)
    mod_all = _modulation(c_all, w_mod, b_mod)

    xc = ctx
    for l in range(depth):
        ctx_used = l <= last_even
        ctx_out = l < last_even
        mod = [mod_all[l, :bsz, k * d:(k + 1) * d].reshape(bsz, 1, d) for k in range(6)]
        mod_c = [mod_all[l, bsz:bsz + 1, k * d:(k + 1) * d].reshape(1, 1, d) for k in range(6)]
        ln = (ln1_g[l], ln1_b[l], ln2_g[l], ln2_b[l])
        w1, w3, w2 = ffn_w1[l].astype(BF16), ffn_w3[l].astype(BF16), ffn_w2[l].astype(BF16)
        j = l // 2
        if l % 2 == 0:
            w_in = w_in_even[j].astype(BF16)
            w_out = w_out_even[j].astype(BF16)
            w_outs = [w_out[:NA_DIM], w_out[NA_DIM:]]
            b_out = jnp.zeros((d,), F32)
            qa, ka, va, qb, ib, zf, zb, gb = _inproj_even(x, mod[0], mod[1], w_in, tm)
            qa_c, ka_c, va_c, qb_c, ib_c, zf_c, zb_c, gb_c = _inproj_even(xc, mod_c[0], mod_c[1], w_in, tmc)
            o_na = _neighbourhood_attention(qa, ka, va, ka_c, va_c, na_rpb[j])
            o_hg, o_hg_c = _hgrn2((qb, ib, zf, zb, gb), (qb_c, ib_c, zf_c, zb_c, gb_c), hg_lower[:, j],
                                  hg_norm_g[j], ctx_out)
            acts = [o_na, o_hg]
            if ctx_out:
                acts_c = [_context_attention(qa_c, ka_c, va_c), o_hg_c]
        else:
            w_outs = [cv_w2[j].astype(BF16)]
            b_out = cv_b2[j]
            conv = (cv_wdw[j], cv_bdw[j], cv_ln_g[j], cv_ln_b[j])
            w_cv1 = cv_w1[j].astype(BF16)
            b_cv1 = cv_b1[j].astype(F32).reshape(1, 2 * d)
            acts = [_conv_module_mid(_inproj_odd(x, mod[0], mod[1], w_cv1, b_cv1, tm), *conv, min(256, t))]
            if ctx_out:
                acts_c = [_conv_module_mid(_inproj_odd(xc, mod_c[0], mod_c[1], w_cv1, b_cv1, tmc), *conv,
                                           min(256, lc))]
        x = _out_ffn(x, acts, w_outs, b_out, mod[2:], ln, w1, w3, w2, tm)
        if ctx_out:
            xc = _out_ffn(xc, acts_c, w_outs, b_out, mod_c[2:], ln, w1, w3, w2, tmc)
    return x
```

```python
import functools

import numpy as np
import jax
import jax.numpy as jnp
from jax import lax
from jax.experimental import pallas as pl
from jax.experimental.pallas import tpu as pltpu

F32 = jnp.float32
BF16 = jnp.bfloat16

D_MODEL = 1024
DEPTH = 4
GRID_W = 64
NA_HEADS = 8
NA_HEAD_DIM = 64
NA_DIM = NA_HEADS * NA_HEAD_DIM
NA_WIN_H = 8
NA_WIN_W = 16
HG_HEADS = 4
HG_HEAD_DIM = 128
HG_DIM = HG_HEADS * HG_HEAD_DIM
HG_CHUNK = 64
HG_LEVELS = 6
HG_GROUP = 32
SUBLANES = 8
HG_MATMUL_LEVEL = 4
LOG2_E = 1.4426950408889634
CONV_WIDTH = 31
CONV_HALO = 16
D_FF = 2816
ALPHA = (2 * DEPTH) ** 0.25
LN_EPS = 1e-5
FFN_ROW_SPLIT = 2
RMS_EPS = 1e-6

VMEM_LIMIT_BYTES = 56 * 1024 * 1024
MOD_ROWS = 24


def _params(n_grid, vmem=VMEM_LIMIT_BYTES):
    return pltpu.CompilerParams(dimension_semantics=("arbitrary",) * n_grid, vmem_limit_bytes=vmem)


def _const_spec(shape):
    nd = len(shape)
    return pl.BlockSpec(shape, lambda *_: (0,) * nd, pipeline_mode=pl.Buffered(1))


def _layer_norm(x, g, b):
    mu = jnp.mean(x, axis=-1, keepdims=True)
    xc = x - mu
    var = jnp.mean(xc * xc, axis=-1, keepdims=True)
    return xc * lax.rsqrt(var + LN_EPS) * g + b


def _silu(x):
    return x * jax.nn.sigmoid(x)


def _dot(a, b):
    return jnp.dot(a, b, preferred_element_type=F32)


def _dot_nt(a, b):
    return lax.dot_general(a, b, (((1,), (1,)), ((), ())), preferred_element_type=F32)


def _dot_tn(a, b):
    return lax.dot_general(a, b, (((0,), (0,)), ((), ())), preferred_element_type=F32)


def _mod_kernel(c_ref, w_ref, b_ref, o_ref):
    s = _silu(c_ref[...]).astype(BF16)
    o_ref[...] = _dot(s, w_ref[...].astype(BF16)) + b_ref[...]


def _modulation(c_all, w_mod, b_mod):
    depth, d, n = w_mod.shape
    tn = 1536
    return pl.pallas_call(
        _mod_kernel,
        grid=(depth, n // tn),
        in_specs=[
            pl.BlockSpec((MOD_ROWS, d), lambda l, j: (0, 0)),
            pl.BlockSpec((None, d, tn), lambda l, j: (l, 0, j)),
            pl.BlockSpec((None, 1, tn), lambda l, j: (l, 0, j)),
        ],
        out_specs=pl.BlockSpec((None, MOD_ROWS, tn), lambda l, j: (l, 0, j)),
        out_shape=jax.ShapeDtypeStruct((depth, MOD_ROWS, n), F32),
        compiler_params=_params(2),
        name="modulation",
    )(c_all, w_mod, b_mod.reshape(depth, 1, n))


def _mod_spec(mod):
    per_batch = 1 if mod.shape[0] > 1 else 0
    return pl.BlockSpec((None, 1, D_MODEL), lambda b, i: (b * per_batch, 0, 0))


def _inproj_even_kernel(x_ref, sh_ref, sc_ref, w_ref, qa_ref, ka_ref, va_ref, qb_ref, ib_ref, zf_ref,
                        zb_ref, gb_ref):
    h = (x_ref[...] * (1.0 + sc_ref[...]) + sh_ref[...]).astype(BF16)
    outs = (qa_ref, ka_ref, va_ref, qb_ref, ib_ref, zf_ref, zb_ref, gb_ref)
    for s, o_ref in enumerate(outs):
        y = _dot(h, w_ref[:, s * NA_DIM:(s + 1) * NA_DIM])
        if s == 0:
            y = y * (NA_HEAD_DIM ** -0.5 * LOG2_E)
        o_ref[...] = y.astype(o_ref.dtype)


def _inproj_even(x, sh, sc, w_bf16, tm):
    bsz, t, d = x.shape
    tok = pl.BlockSpec((None, tm, d), lambda b, i: (b, i, 0))
    out_spec = pl.BlockSpec((None, tm, NA_DIM), lambda b, i: (b, i, 0))
    dtypes = (BF16, BF16, BF16, F32, BF16, F32, F32, F32)
    return pl.pallas_call(
        _inproj_even_kernel,
        grid=(bsz, t // tm),
        in_specs=[tok, _mod_spec(sh), _mod_spec(sc), _const_spec(w_bf16.shape)],
        out_specs=[out_spec] * 8,
        out_shape=[jax.ShapeDtypeStruct((bsz, t, NA_DIM), dt) for dt in dtypes],
        compiler_params=_params(2),
        name="inproj_even",
    )(x, sh, sc, w_bf16)


def _inproj_odd_kernel(x_ref, sh_ref, sc_ref, w_ref, b_ref, u_ref):
    h = (x_ref[...] * (1.0 + sc_ref[...]) + sh_ref[...]).astype(BF16)
    a = _dot(h, w_ref[:, :D_MODEL]) + b_ref[:, :D_MODEL]
    g = _dot(h, w_ref[:, D_MODEL:]) + b_ref[:, D_MODEL:]
    u_ref[...] = a * jax.nn.sigmoid(g)


def _inproj_odd(x, sh, sc, w_bf16, bias, tm):
    bsz, t, d = x.shape
    tok = pl.BlockSpec((None, tm, d), lambda b, i: (b, i, 0))
    return pl.pallas_call(
        _inproj_odd_kernel,
        grid=(bsz, t // tm),
        in_specs=[tok, _mod_spec(sh), _mod_spec(sc), _const_spec(w_bf16.shape), _const_spec(bias.shape)],
        out_specs=tok,
        out_shape=jax.ShapeDtypeStruct((bsz, t, d), F32),
        compiler_params=_params(2),
        name="inproj_odd",
    )(x, sh, sc, w_bf16, bias)


HEADS_PER_GROUP = 4
GROUP_W = HEADS_PER_GROUP * NA_HEAD_DIM
NA_TILE_ROWS = 4
NA_TILES_PER_STEP = 2


def _softmax_pv(s_parts, v_parts):
    m = s_parts[0].max(axis=-1, keepdims=True)
    for s in s_parts[1:]:
        m = jnp.maximum(m, s.max(axis=-1, keepdims=True))
    ps = [jnp.exp2(s - m) for s in s_parts]
    l = ps[0].sum(axis=-1, keepdims=True)
    for p in ps[1:]:
        l = l + p.sum(axis=-1, keepdims=True)
    o = _dot(ps[0].astype(BF16), v_parts[0])
    for p, v in zip(ps[1:], v_parts[1:]):
        o = o + _dot(p.astype(BF16), v)
    return o * (1.0 / l)


def _na_kernel(q_ref, k_ref, v_ref, kc_ref, vc_ref, bias_ref, o_ref, *, rows, key_rows):
    n_tiles = rows // NA_TILE_ROWS
    nq = NA_TILE_ROWS * GRID_W
    nk = key_rows * GRID_W
    lane = lax.broadcasted_iota(jnp.int32, (nq, GROUP_W), 1)
    head_lanes = [(lane >= hh * NA_HEAD_DIM) & (lane < (hh + 1) * NA_HEAD_DIM) for hh in range(HEADS_PER_GROUP)]
    for t in range(NA_TILES_PER_STEP):
        tile = pl.program_id(1) * NA_TILES_PER_STEP + t
        first_row = jnp.clip(tile * NA_TILE_ROWS - NA_WIN_H // 2, 0, rows - key_rows)
        start = pl.multiple_of(first_row * GRID_W, GRID_W)
        pattern = jnp.minimum(tile, 1) + (tile == n_tiles - 1).astype(jnp.int32)
        q_rows = slice(t * nq, (t + 1) * nq)
        for g in range(NA_HEADS // HEADS_PER_GROUP):
            cols = slice(g * GROUP_W, (g + 1) * GROUP_W)
            qg = q_ref[q_rows, cols]
            kg = k_ref[pl.ds(start, nk), cols]
            vg = v_ref[pl.ds(start, nk), cols]
            kcg = kc_ref[:, cols]
            vcg = vc_ref[:, cols]
            scores = []
            for hh in range(HEADS_PER_GROUP):
                qh = jnp.where(head_lanes[hh], qg, jnp.zeros_like(qg))
                scores.append([_dot_nt(qh, kg) + bias_ref[pattern, g * HEADS_PER_GROUP + hh], _dot_nt(qh, kcg)])
            acc = jnp.zeros((nq, GROUP_W), F32)
            for hh in range(HEADS_PER_GROUP):
                acc = jnp.where(head_lanes[hh], _softmax_pv(scores[hh], [vg, vcg]), acc)
            o_ref[q_rows, cols] = acc.astype(o_ref.dtype)


def _na_tile_geometry(rows):
    kr = min(NA_WIN_H, rows)
    key_rows = min(NA_TILE_ROWS + NA_WIN_H, rows)
    outside = 2 * NA_WIN_H - 1
    geo = []
    for r0 in range(0, rows, NA_TILE_ROWS):
        first = int(np.clip(r0 - NA_WIN_H // 2, 0, rows - key_rows))
        idx = np.full((NA_TILE_ROWS, key_rows), outside, np.int32)
        for i in range(NA_TILE_ROWS):
            r = r0 + i
            rs = int(np.clip(r - kr // 2, 0, rows - kr))
            assert first <= rs and rs + kr <= first + key_rows
            for j in range(rs - first, rs - first + kr):
                idx[i, j] = first + j - r + NA_WIN_H - 1
        geo.append(idx)
    return key_rows, geo


def _na_bias_table(rpb, rows):
    key_rows, geo = _na_tile_geometry(rows)
    assert all(np.array_equal(g, geo[1]) for g in geo[1:-1])
    col = np.arange(GRID_W)
    col_start = np.clip(col - NA_WIN_W // 2, 0, GRID_W - NA_WIN_W)
    col_in = (col[None, :] >= col_start[:, None]) & (col[None, :] < col_start[:, None] + NA_WIN_W)
    dc_idx = np.clip(col[None, :] - col[:, None] + NA_WIN_W - 1, 0, 2 * NA_WIN_W - 2)
    per_dr = jnp.where(col_in[None, None], rpb[:, :, dc_idx].astype(F32) * LOG2_E, -jnp.inf)
    per_dr = jnp.concatenate([per_dr, jnp.full_like(per_dr[:, :1], -jnp.inf)], axis=1)
    tabs = []
    for idx in (geo[0], geo[1], geo[-1]):
        blocks = per_dr[:, idx]
        tabs.append(jnp.transpose(blocks, (0, 1, 3, 2, 4)).reshape(
            NA_HEADS, NA_TILE_ROWS * GRID_W, key_rows * GRID_W))
    return key_rows, jnp.stack(tabs)


def _neighbourhood_attention(q, k, v, kc, vc, rpb):
    bsz, t, _ = q.shape
    lc = kc.shape[1]
    rows = t // GRID_W
    n_tiles = rows // NA_TILE_ROWS
    assert n_tiles >= 3 and n_tiles % NA_TILES_PER_STEP == 0
    key_rows, bias = _na_bias_table(rpb, rows)
    nq = NA_TILES_PER_STEP * NA_TILE_ROWS * GRID_W
    tile_spec = pl.BlockSpec((None, nq, NA_DIM), lambda b, i: (b, i, 0))
    seq_spec = pl.BlockSpec((None, t, NA_DIM), lambda b, i: (b, 0, 0))
    ctx_spec = pl.BlockSpec((None, lc, NA_DIM), lambda b, i: (b, 0, 0))
    return pl.pallas_call(
        functools.partial(_na_kernel, rows=rows, key_rows=key_rows),
        grid=(bsz, n_tiles // NA_TILES_PER_STEP),
        in_specs=[tile_spec, seq_spec, seq_spec, ctx_spec, ctx_spec, _const_spec(bias.shape)],
        out_specs=tile_spec,
        out_shape=jax.ShapeDtypeStruct((bsz, t, NA_DIM), BF16),
        compiler_params=_params(2),
        name="neighbourhood_attention",
    )(q, k, v, kc, vc, bias)


def _ctx_attn_kernel(q_ref, k_ref, v_ref, o_ref):
    nq = q_ref.shape[0]
    lane = lax.broadcasted_iota(jnp.int32, (nq, GROUP_W), 1)
    for g in range(NA_HEADS // HEADS_PER_GROUP):
        cols = slice(g * GROUP_W, (g + 1) * GROUP_W)
        qg = q_ref[:, cols]
        kg = k_ref[:, cols]
        vg = v_ref[:, cols]
        acc = jnp.zeros((nq, GROUP_W), F32)
        for hh in range(HEADS_PER_GROUP):
            head_lanes = (lane >= hh * NA_HEAD_DIM) & (lane < (hh + 1) * NA_HEAD_DIM)
            qh = jnp.where(head_lanes, qg, jnp.zeros_like(qg))
            o = _softmax_pv([_dot_nt(qh, kg)], [vg])
            acc = jnp.where(head_lanes, o, acc)
        o_ref[:, cols] = acc.astype(o_ref.dtype)


def _context_attention(q, k, v):
    bsz, lc, _ = q.shape
    spec = pl.BlockSpec((None, lc, NA_DIM), lambda b: (b, 0, 0))
    return pl.pallas_call(
        _ctx_attn_kernel,
        grid=(bsz,),
        in_specs=[spec, spec, spec],
        out_specs=spec,
        out_shape=jax.ShapeDtypeStruct((bsz, lc, NA_DIM), BF16),
        compiler_params=_params(1),
        name="context_attention",
    )(q, k, v)


def _hgrn_tables():
    L = HG_CHUNK
    t = np.arange(L)
    p_rows, masks = [], []
    for lev in range(HG_LEVELS):
        c = L >> lev
        mid = (t // c) * c + c // 2
        P = np.zeros((L, L), np.float32)
        for i in range(L):
            m = mid[i]
            if i >= m:
                P[i, m + 1:i + 1] = 1.0
            else:
                P[i, i + 1:m + 1] = 1.0
        p_rows.append(P)
        same = (t[:, None] // c) == (t[None, :] // c)
        masks.append((same & (t[:, None] >= mid[:, None]) & (t[None, :] < mid[None, :])).astype(np.float32))
    incl = np.tril(np.ones((L, L), np.float32))
    fine = p_rows[HG_MATMUL_LEVEL:]
    pf = np.concatenate([incl] + fine, axis=0)
    flip = lambda a: a[..., ::-1, ::-1]
    pb = np.concatenate([flip(incl)] + [flip(p) for p in fine], axis=0)
    side_by_side = lambda ms: np.stack([np.concatenate([m, m], axis=1) for m in ms])
    mf = side_by_side(masks)
    mb = side_by_side([flip(m) for m in masks])
    return pf, pb, mf, mb


def _hgrn_gates(z, log_lb, log1m_lb, one_m_lb):
    e = jnp.exp(-jnp.abs(z))
    inv = 1.0 / (1.0 + e)
    key = one_m_lb * jnp.where(z >= 0, e * inv, inv)
    c = log1m_lb + (jnp.minimum(z, 0.0) - jnp.log(1.0 + e))
    log_f = jnp.maximum(log_lb, c) + jnp.log(1.0 + jnp.exp(-jnp.abs(log_lb - c)))
    return key, log_f


def _hgrn_group(q_ref, v_ref, z_ref, first_chunk, G, lbp, p_ref, m_ref, st_ref, reverse, emit):
    L, dk = HG_CHUNK, HG_HEAD_DIM
    base = pl.multiple_of(first_chunk * L, G * L)
    group_rows = pl.ds(base, G * L)
    chunk = lambda a, j: a[j * L:(j + 1) * L]
    wide = lambda a: jnp.concatenate([chunk(a, j) for j in range(G)], axis=1)

    q, v, z = q_ref[group_rows, :], v_ref[group_rows, :], z_ref[group_rows, :]
    key, log_f = _hgrn_gates(z, lbp[0], lbp[1], lbp[2])
    log2_f = log_f * LOG2_E
    g_hi = log2_f.astype(BF16)
    g_lo = (log2_f - g_hi.astype(F32)).astype(BF16)
    p = p_ref[...]
    x = _dot(p, wide(g_hi)) + _dot(p, wide(g_lo))
    cum = x[:L]

    def level_exponent(c):
        pieces = []
        for i in range(L // c):
            mid = i * c + c // 2 - (1 if reverse else 0)
            ref = cum[mid:mid + 1]
            blk = cum[i * c:(i + 1) * c]
            if c >= 2 * SUBLANES:
                first_half, second_half = blk[:c // 2], blk[c // 2:]
                pieces += ([first_half - ref, ref - second_half] if reverse
                           else [ref - first_half, second_half - ref])
            else:
                pieces.append(-jnp.abs(blk - ref))
        return jnp.concatenate(pieces, axis=0)

    last_row = 0 if reverse else L - 1
    exponents = ([level_exponent(L >> lev) for lev in range(HG_MATMUL_LEVEL)]
                 + [x[(1 + i) * L:(2 + i) * L] for i in range(HG_LEVELS - HG_MATMUL_LEVEL)]
                 + [cum, cum[last_row:last_row + 1] - cum])
    decay = [jnp.exp2(e) for e in exponents]
    decay16 = [d.astype(BF16) for d in decay]
    dec16 = lambda blk, j: decay16[blk][:, j * dk:(j + 1) * dk]
    q16, key16 = q.astype(BF16), key.astype(BF16)
    qs = [chunk(q16, j) for j in range(G)]
    ks = [chunk(key16, j) for j in range(G)]
    vs = [chunk(v, j) for j in range(G)]
    qk = jnp.sum(q * key, axis=-1, keepdims=True)

    zero = jnp.zeros((L, dk), BF16)
    block_diag = lambda a, b: jnp.concatenate([jnp.concatenate([a, zero], axis=1),
                                               jnp.concatenate([zero, b], axis=1)], axis=0)
    row = lax.broadcasted_iota(jnp.int32, (L, 2 * L), 0)
    col = lax.broadcasted_iota(jnp.int32, (L, 2 * L), 1)
    pairs = range(0, G, 2)
    a = [jnp.where(col == row, chunk(qk, j), jnp.where(col == row + L, chunk(qk, j + 1), 0.0)) for j in pairs]
    for lev in range(HG_LEVELS):
        for i, j in enumerate(pairs):
            e0, e1 = dec16(lev, j), dec16(lev, j + 1)
            q_pair = jnp.concatenate([qs[j] * e0, qs[j + 1] * e1], axis=1)
            a[i] = a[i] + m_ref[lev] * _dot_nt(q_pair, block_diag(ks[j] * e0, ks[j + 1] * e1))
    o_local = []
    for i, j in enumerate(pairs):
        o_pair = _dot(a[i].astype(BF16), block_diag(vs[j], vs[j + 1]))
        o_local += [o_pair[:, :dk], o_pair[:, dk:]]
    q_in = [qs[j] * dec16(HG_LEVELS, j) for j in range(G)]
    st_add = [_dot_tn(vs[j], ks[j] * dec16(HG_LEVELS + 1, j)) for j in range(G)]
    e_all = [decay[HG_LEVELS][last_row:last_row + 1, j * dk:(j + 1) * dk] for j in range(G)]

    st = st_ref[...]
    for j in (reversed(range(G)) if reverse else range(G)):
        emit(pl.ds(base + j * L, L), o_local[j] + _dot_nt(q_in[j], st.astype(BF16)))
        st = st * e_all[j] + st_add[j]
    st_ref[...] = st


def _hgrn_kernel(q_ref, v_ref, zf_ref, zb_ref, g_ref, qc_ref, vc_ref, zfc_ref, zbc_ref, gc_ref,
                 lb_ref, ng_ref, pf_ref, pb_ref, mf_ref, mb_ref, *rest, ctx_out):
    if ctx_out:
        o_ref, oc_ref, st_ref, of_ref, ofc_ref = rest
    else:
        o_ref, st_ref, of_ref = rest
        oc_ref = ofc_ref = None
    norm_g = ng_ref[...]

    def readout(o, g):
        o = o * lax.rsqrt(jnp.mean(o * o, axis=-1, keepdims=True) + RMS_EPS) * norm_g
        return (o * _silu(g)).astype(BF16)

    def store_to(dst_ref):
        def emit(rows, o):
            dst_ref[rows, :] = o
        return emit

    def readout_to(dst_ref, fwd_ref, gate_ref):
        def emit(rows, o):
            dst_ref[rows, :] = readout(fwd_ref[rows, :] + o, gate_ref[rows, :])
        return emit

    def drop(rows, o):
        pass

    def scan(refs, lbp, p_ref, m_ref, reverse, emit):
        n_chunks = refs[0].shape[0] // HG_CHUNK
        group = min(HG_GROUP, n_chunks)
        n_groups = n_chunks // group

        def body(i, carry):
            first_chunk = ((n_groups - 1 - i) if reverse else i) * group
            _hgrn_group(*refs, first_chunk, group, lbp, p_ref, m_ref, st_ref, reverse, emit)
            return carry
        lax.fori_loop(0, n_groups, body, 0)

    lbf = (lb_ref[0, 0], lb_ref[0, 1], lb_ref[0, 2])
    st_ref[...] = jnp.zeros_like(st_ref)
    scan((qc_ref, vc_ref, zfc_ref), lbf, pf_ref, mf_ref, False, store_to(ofc_ref) if ctx_out else drop)
    scan((q_ref, v_ref, zf_ref), lbf, pf_ref, mf_ref, False, store_to(of_ref))

    lbb = (lb_ref[1, 0], lb_ref[1, 1], lb_ref[1, 2])
    st_ref[...] = jnp.zeros_like(st_ref)
    scan((qc_ref, vc_ref, zbc_ref), lbb, pb_ref, mb_ref, True,
         readout_to(oc_ref, ofc_ref, gc_ref) if ctx_out else drop)
    scan((q_ref, v_ref, zb_ref), lbb, pb_ref, mb_ref, True, readout_to(o_ref, of_ref, g_ref))


def _hgrn2(main, ctx, lb, norm_g, ctx_out):
    q = main[0]
    bsz, t, _ = q.shape
    lc = ctx[0].shape[1]
    pf, pb, mf, mb = _hgrn_tables()
    lb = lb.astype(F32)
    lb_params = jnp.stack([jnp.log(lb), jnp.log1p(-lb), 1.0 - lb], axis=1)
    lb_params = jnp.transpose(lb_params, (2, 0, 1, 3))[:, :, :, None, :]
    seq_spec = pl.BlockSpec((None, t, HG_HEAD_DIM), lambda b, h: (b, 0, h))
    ctx_spec = pl.BlockSpec((None, lc, HG_HEAD_DIM), lambda b, h: (b, 0, h))
    in_specs = ([seq_spec] * 5 + [ctx_spec] * 5 + [
        pl.BlockSpec((None, 2, 3, 1, HG_HEAD_DIM), lambda b, h: (h, 0, 0, 0, 0)),
        pl.BlockSpec((None, 1, HG_HEAD_DIM), lambda b, h: (h, 0, 0)),
        _const_spec(pf.shape), _const_spec(pb.shape), _const_spec(mf.shape), _const_spec(mb.shape)])
    out_specs = [seq_spec]
    out_shape = [jax.ShapeDtypeStruct((bsz, t, HG_DIM), BF16)]
    scratch = [pltpu.VMEM((HG_HEAD_DIM, HG_HEAD_DIM), F32), pltpu.VMEM((t, HG_HEAD_DIM), F32)]
    if ctx_out:
        out_specs.append(ctx_spec)
        out_shape.append(jax.ShapeDtypeStruct((bsz, lc, HG_DIM), BF16))
        scratch.append(pltpu.VMEM((lc, HG_HEAD_DIM), F32))
    outs = pl.pallas_call(
        functools.partial(_hgrn_kernel, ctx_out=ctx_out),
        grid=(bsz, HG_HEADS),
        in_specs=in_specs,
        out_specs=out_specs,
        out_shape=out_shape,
        scratch_shapes=scratch,
        compiler_params=_params(2),
        name="hgrn2",
    )(*main, *ctx, lb_params, norm_g.astype(F32)[:, None, :],
      jnp.asarray(pf, BF16), jnp.asarray(pb, BF16), jnp.asarray(mf), jnp.asarray(mb))
    return (outs[0], outs[1]) if ctx_out else (outs[0], None)


LANE_BLOCK = 128


def _conv_kernel(prev_ref, cur_ref, next_ref, w_ref, b_ref, g_ref, beta_ref, o_ref, win_ref, acc_ref,
                 shift_ref):
    i = pl.program_id(1)
    tm = cur_ref.shape[0]
    win_ref[0:CONV_HALO, :] = jnp.where(i > 0, prev_ref[...], 0.0)
    win_ref[CONV_HALO:CONV_HALO + tm, :] = cur_ref[...]
    win_ref[CONV_HALO + tm:, :] = jnp.where(i < pl.num_programs(1) - 1, next_ref[...], 0.0)
    first = CONV_HALO - CONV_WIDTH // 2
    span = tm + 2 * CONV_HALO - SUBLANES
    def lane_block(c, carry):
        cols = pl.ds(pl.multiple_of(c * LANE_BLOCK, LANE_BLOCK), LANE_BLOCK)
        for phase in range(SUBLANES):
            shift_ref[phase] = win_ref[phase:phase + span, cols]
        acc = jnp.zeros((tm, LANE_BLOCK), F32)
        for k in range(CONV_WIDTH):
            phase = (first + k) % SUBLANES
            off = first + k - phase
            acc = acc + shift_ref[phase, off:off + tm, :] * w_ref[k, :, cols]
        acc_ref[:, cols] = acc + b_ref[:, cols]
        return carry

    lax.fori_loop(0, D_MODEL // LANE_BLOCK, lane_block, 0)
    y = _layer_norm(acc_ref[...], g_ref[...], beta_ref[...])
    o_ref[...] = _silu(y).astype(o_ref.dtype)


def _conv_module_mid(u, w_dw, b_dw, ln_g, ln_b, tm):
    bsz, t, d = u.shape
    hb = tm // CONV_HALO
    n_hb = t // CONV_HALO
    vec = lambda a: a.astype(F32).reshape(1, d)
    w_taps = w_dw.astype(F32).reshape(CONV_WIDTH, 1, d)
    return pl.pallas_call(
        _conv_kernel,
        grid=(bsz, t // tm),
        in_specs=[
            pl.BlockSpec((None, CONV_HALO, d), lambda b, i: (b, jnp.maximum(i * hb - 1, 0), 0)),
            pl.BlockSpec((None, tm, d), lambda b, i: (b, i, 0)),
            pl.BlockSpec((None, CONV_HALO, d), lambda b, i: (b, jnp.minimum((i + 1) * hb, n_hb - 1), 0)),
            _const_spec((CONV_WIDTH, 1, d)), _const_spec((1, d)), _const_spec((1, d)), _const_spec((1, d)),
        ],
        out_specs=pl.BlockSpec((None, tm, d), lambda b, i: (b, i, 0)),
        out_shape=jax.ShapeDtypeStruct((bsz, t, d), BF16),
        scratch_shapes=[pltpu.VMEM((tm + 2 * CONV_HALO, d), F32), pltpu.VMEM((tm, d), F32),
                        pltpu.VMEM((SUBLANES, tm + 2 * CONV_HALO - SUBLANES, LANE_BLOCK), F32)],
        compiler_params=_params(2),
        name="conv_module",
    )(u, u, u, w_taps, vec(b_dw), vec(ln_g), vec(ln_b))


def _ffn_kernel(*refs, n_act):
    x_ref = refs[0]
    act_refs = refs[1:1 + n_act]
    wo_refs = refs[1 + n_act:1 + 2 * n_act]
    (bo_ref, g1_ref, sh2_ref, sc2_ref, g2_ref, l1g_ref, l1b_ref, l2g_ref, l2b_ref,
     w1_ref, w3_ref, w2_ref, o_ref) = refs[1 + 2 * n_act:]
    sub = x_ref.shape[0] // FFN_ROW_SPLIT
    rows = [slice(r * sub, (r + 1) * sub) for r in range(FFN_ROW_SPLIT)]

    def out_proj(r):
        y = bo_ref[...]
        for a_ref, w_ref in zip(act_refs, wo_refs):
            y = y + _dot(a_ref[rows[r], :], w_ref[...])
        return y

    def norm1(r, y):
        x1 = _layer_norm(ALPHA * x_ref[rows[r], :] + g1_ref[...] * y, l1g_ref[...], l1b_ref[...])
        return x1, (x1 * (1.0 + sc2_ref[...]) + sh2_ref[...]).astype(BF16)

    def up(r, h):
        return _dot(h, w1_ref[...]), _dot(h, w3_ref[...])

    def gate(r, h1, h3):
        return (_silu(h1) * h3).astype(BF16)

    def down(r, u):
        return _dot(u, w2_ref[...])

    def norm2(r, x1, f):
        o_ref[rows[r], :] = _layer_norm(ALPHA * x1 + g2_ref[...] * f, l2g_ref[...], l2b_ref[...])

    stages = 6
    state = [None] * FFN_ROW_SPLIT
    for step in range(stages + FFN_ROW_SPLIT - 1):
        for r in range(FFN_ROW_SPLIT):
            stage = step - r
            if stage == 0:
                state[r] = out_proj(r)
            elif stage == 1:
                state[r] = norm1(r, state[r])
            elif stage == 2:
                x1, h = state[r]
                state[r] = (x1, up(r, h))
            elif stage == 3:
                x1, (h1, h3) = state[r]
                state[r] = (x1, gate(r, h1, h3))
            elif stage == 4:
                x1, u = state[r]
                state[r] = (x1, down(r, u))
            elif stage == 5:
                norm2(r, *state[r])


def _out_ffn(x, acts, w_outs, b_out, mods, ln, w1, w3, w2, tm):
    bsz, t, d = x.shape
    n_act = len(acts)
    tok = lambda width: pl.BlockSpec((None, tm, width), lambda b, i: (b, i, 0))
    vec = lambda a: a.astype(F32).reshape(1, d)
    g1, sh2, sc2, g2 = mods
    in_specs = ([tok(d)] + [tok(a.shape[-1]) for a in acts] + [_const_spec(w.shape) for w in w_outs]
                + [_const_spec((1, d))] + [_mod_spec(m) for m in mods] + [_const_spec((1, d))] * 4
                + [_const_spec(w1.shape), _const_spec(w3.shape), _const_spec(w2.shape)])
    return pl.pallas_call(
        functools.partial(_ffn_kernel, n_act=n_act),
        grid=(bsz, t // tm),
        in_specs=in_specs,
        out_specs=tok(d),
        out_shape=jax.ShapeDtypeStruct((bsz, t, d), F32),
        compiler_params=_params(2),
        name="out_ffn",
    )(x, *acts, *w_outs, vec(b_out), g1, sh2, sc2, g2, *[vec(a) for a in ln], w1, w3, w2)


def kernel(x, c, ctx, c_ctx, w_mod, b_mod, ln1_g, ln1_b, ln2_g, ln2_b, w_in_even, w_out_even, na_rpb, hg_lb,
           hg_norm_g, cv_w1, cv_b1, cv_wdw, cv_bdw, cv_ln_g, cv_ln_b, cv_w2, cv_b2, ffn_w1, ffn_w3, ffn_w2):
    bsz, t, d = x.shape
    lc = ctx.shape[1]
    depth = w_mod.shape[0]
    last_even = depth - 1 if (depth - 1) % 2 == 0 else depth - 2
    tm = min(512, t)
    tmc = min(512, lc)

    p_lb = jax.nn.softmax(hg_lb.astype(F32), axis=1)
    hg_lower = jnp.maximum(jnp.cumsum(p_lb, axis=1) - p_lb[:, :1], 0.0)

    c_all = jnp.zeros((MOD_ROWS, d), F32).at[:bsz].set(c).at[bsz].set(c_ctx)
    mod_all = _modulation(c_all, w_mod, b_mod)

    xc = ctx
    for l in range(depth):
        ctx_used = l <= last_even
        ctx_out = l < last_even
        mod = [mod_all[l, :bsz, k * d:(k + 1) * d].reshape(bsz, 1, d) for k in range(6)]
        mod_c = [mod_all[l, bsz:bsz + 1, k * d:(k + 1) * d].reshape(1, 1, d) for k in range(6)]
        ln = (ln1_g[l], ln1_b[l], ln2_g[l], ln2_b[l])
        w1, w3, w2 = ffn_w1[l].astype(BF16), ffn_w3[l].astype(BF16), ffn_w2[l].astype(BF16)
        j = l // 2
        if l % 2 == 0:
            w_in = w_in_even[j].astype(BF16)
            w_out = w_out_even[j].astype(BF16)
            w_outs = [w_out[:NA_DIM], w_out[NA_DIM:]]
            b_out = jnp.zeros((d,), F32)
            qa, ka, va, qb, ib, zf, zb, gb = _inproj_even(x, mod[0], mod[1], w_in, tm)
            qa_c, ka_c, va_c, qb_c, ib_c, zf_c, zb_c, gb_c = _inproj_even(xc, mod_c[0], mod_c[1], w_in, tmc)
            o_na = _neighbourhood_attention(qa, ka, va, ka_c, va_c, na_rpb[j])
            o_hg, o_hg_c = _hgrn2((qb, ib, zf, zb, gb), (qb_c, ib_c, zf_c, zb_c, gb_c), hg_lower[:, j],
                                  hg_norm_g[j], ctx_out)
            acts = [o_na, o_hg]
            if ctx_out:
                acts_c = [_context_attention(qa_c, ka_c, va_c), o_hg_c]
        else:
            w_outs = [cv_w2[j].astype(BF16)]
            b_out = cv_b2[j]
            conv = (cv_wdw[j], cv_bdw[j], cv_ln_g[j], cv_ln_b[j])
            w_cv1 = cv_w1[j].astype(BF16)
            b_cv1 = cv_b1[j].astype(F32).reshape(1, 2 * d)
            acts = [_conv_module_mid(_inproj_odd(x, mod[0], mod[1], w_cv1, b_cv1, tm), *conv, min(256, t))]
            if ctx_out:
                acts_c = [_conv_module_mid(_inproj_odd(xc, mod_c[0], mod_c[1], w_cv1, b_cv1, tmc), *conv,
                                           min(256, lc))]
        x = _out_ffn(x, acts, w_outs, b_out, mod[2:], ln, w1, w3, w2, tm)
        if ctx_out:
            xc = _out_ffn(xc, acts_c, w_outs, b_out, mod_c[2:], ln, w1, w3, w2, tmc)
    return x
```

```python
import functools

import numpy as np
import jax
import jax.numpy as jnp
from jax import lax
from jax.experimental import pallas as pl
from jax.experimental.pallas import tpu as pltpu

F32 = jnp.float32
BF16 = jnp.bfloat16

D_MODEL = 1024
DEPTH = 4
GRID_W = 64
NA_HEADS = 8
NA_HEAD_DIM = 64
NA_DIM = NA_HEADS * NA_HEAD_DIM
NA_WIN_H = 8
NA_WIN_W = 16
HG_HEADS = 4
HG_HEAD_DIM = 128
HG_DIM = HG_HEADS * HG_HEAD_DIM
HG_CHUNK = 64
HG_LEVELS = 6
HG_GROUP = 64
SUBLANES = 8
HG_MATMUL_LEVEL = 4
LOG2_E = 1.4426950408889634
CONV_WIDTH = 31
CONV_HALO = 16
D_FF = 2816
ALPHA = (2 * DEPTH) ** 0.25
LN_EPS = 1e-5
FFN_ROW_SPLIT = 2
RMS_EPS = 1e-6

VMEM_LIMIT_BYTES = 56 * 1024 * 1024
MOD_ROWS = 24


def _params(n_grid, vmem=VMEM_LIMIT_BYTES):
    return pltpu.CompilerParams(dimension_semantics=("arbitrary",) * n_grid, vmem_limit_bytes=vmem)


def _const_spec(shape):
    nd = len(shape)
    return pl.BlockSpec(shape, lambda *_: (0,) * nd, pipeline_mode=pl.Buffered(1))


def _layer_norm(x, g, b):
    mu = jnp.mean(x, axis=-1, keepdims=True)
    xc = x - mu
    var = jnp.mean(xc * xc, axis=-1, keepdims=True)
    return xc * lax.rsqrt(var + LN_EPS) * g + b


def _silu(x):
    return x * jax.nn.sigmoid(x)


def _dot(a, b):
    return jnp.dot(a, b, preferred_element_type=F32)


def _dot_nt(a, b):
    return lax.dot_general(a, b, (((1,), (1,)), ((), ())), preferred_element_type=F32)


def _dot_tn(a, b):
    return lax.dot_general(a, b, (((0,), (0,)), ((), ())), preferred_element_type=F32)


def _mod_kernel(c_ref, w_ref, b_ref, o_ref):
    s = _silu(c_ref[...]).astype(BF16)
    o_ref[...] = _dot(s, w_ref[...].astype(BF16)) + b_ref[...]


def _modulation(c_all, w_mod, b_mod):
    depth, d, n = w_mod.shape
    tn = 1536
    return pl.pallas_call(
        _mod_kernel,
        grid=(depth, n // tn),
        in_specs=[
            pl.BlockSpec((MOD_ROWS, d), lambda l, j: (0, 0)),
            pl.BlockSpec((None, d, tn), lambda l, j: (l, 0, j)),
            pl.BlockSpec((None, 1, tn), lambda l, j: (l, 0, j)),
        ],
        out_specs=pl.BlockSpec((None, MOD_ROWS, tn), lambda l, j: (l, 0, j)),
        out_shape=jax.ShapeDtypeStruct((depth, MOD_ROWS, n), F32),
        compiler_params=_params(2),
        name="modulation",
    )(c_all, w_mod, b_mod.reshape(depth, 1, n))


def _mod_spec(mod):
    per_batch = 1 if mod.shape[0] > 1 else 0
    return pl.BlockSpec((None, 1, D_MODEL), lambda b, i: (b * per_batch, 0, 0))


def _inproj_even_kernel(x_ref, sh_ref, sc_ref, w_ref, qa_ref, ka_ref, va_ref, qb_ref, ib_ref, zf_ref,
                        zb_ref, gb_ref):
    h = (x_ref[...] * (1.0 + sc_ref[...]) + sh_ref[...]).astype(BF16)
    outs = (qa_ref, ka_ref, va_ref, qb_ref, ib_ref, zf_ref, zb_ref, gb_ref)
    for s, o_ref in enumerate(outs):
        y = _dot(h, w_ref[:, s * NA_DIM:(s + 1) * NA_DIM])
        if s == 0:
            y = y * (NA_HEAD_DIM ** -0.5 * LOG2_E)
        o_ref[...] = y.astype(o_ref.dtype)


def _inproj_even(x, sh, sc, w_bf16, tm):
    bsz, t, d = x.shape
    tok = pl.BlockSpec((None, tm, d), lambda b, i: (b, i, 0))
    out_spec = pl.BlockSpec((None, tm, NA_DIM), lambda b, i: (b, i, 0))
    dtypes = (BF16, BF16, BF16, F32, BF16, F32, F32, F32)
    return pl.pallas_call(
        _inproj_even_kernel,
        grid=(bsz, t // tm),
        in_specs=[tok, _mod_spec(sh), _mod_spec(sc), _const_spec(w_bf16.shape)],
        out_specs=[out_spec] * 8,
        out_shape=[jax.ShapeDtypeStruct((bsz, t, NA_DIM), dt) for dt in dtypes],
        compiler_params=_params(2),
        name="inproj_even",
    )(x, sh, sc, w_bf16)


def _inproj_odd_kernel(x_ref, sh_ref, sc_ref, w_ref, b_ref, u_ref):
    h = (x_ref[...] * (1.0 + sc_ref[...]) + sh_ref[...]).astype(BF16)
    a = _dot(h, w_ref[:, :D_MODEL]) + b_ref[:, :D_MODEL]
    g = _dot(h, w_ref[:, D_MODEL:]) + b_ref[:, D_MODEL:]
    u_ref[...] = a * jax.nn.sigmoid(g)


def _inproj_odd(x, sh, sc, w_bf16, bias, tm):
    bsz, t, d = x.shape
    tok = pl.BlockSpec((None, tm, d), lambda b, i: (b, i, 0))
    return pl.pallas_call(
        _inproj_odd_kernel,
        grid=(bsz, t // tm),
        in_specs=[tok, _mod_spec(sh), _mod_spec(sc), _const_spec(w_bf16.shape), _const_spec(bias.shape)],
        out_specs=tok,
        out_shape=jax.ShapeDtypeStruct((bsz, t, d), F32),
        compiler_params=_params(2),
        name="inproj_odd",
    )(x, sh, sc, w_bf16, bias)


HEADS_PER_GROUP = 4
GROUP_W = HEADS_PER_GROUP * NA_HEAD_DIM
NA_TILE_ROWS = 4
NA_TILES_PER_STEP = 2


def _softmax_pv(s_parts, v_parts):
    m = s_parts[0].max(axis=-1, keepdims=True)
    for s in s_parts[1:]:
        m = jnp.maximum(m, s.max(axis=-1, keepdims=True))
    ps = [jnp.exp2(s - m) for s in s_parts]
    l = ps[0].sum(axis=-1, keepdims=True)
    for p in ps[1:]:
        l = l + p.sum(axis=-1, keepdims=True)
    o = _dot(ps[0].astype(BF16), v_parts[0])
    for p, v in zip(ps[1:], v_parts[1:]):
        o = o + _dot(p.astype(BF16), v)
    return o * (1.0 / l)


def _na_kernel(q_ref, k_ref, v_ref, kc_ref, vc_ref, bias_ref, o_ref, *, rows, key_rows):
    n_tiles = rows // NA_TILE_ROWS
    nq = NA_TILE_ROWS * GRID_W
    nk = key_rows * GRID_W
    lane = lax.broadcasted_iota(jnp.int32, (nq, GROUP_W), 1)
    head_lanes = [(lane >= hh * NA_HEAD_DIM) & (lane < (hh + 1) * NA_HEAD_DIM) for hh in range(HEADS_PER_GROUP)]
    for t in range(NA_TILES_PER_STEP):
        tile = pl.program_id(1) * NA_TILES_PER_STEP + t
        first_row = jnp.clip(tile * NA_TILE_ROWS - NA_WIN_H // 2, 0, rows - key_rows)
        start = pl.multiple_of(first_row * GRID_W, GRID_W)
        pattern = jnp.minimum(tile, 1) + (tile == n_tiles - 1).astype(jnp.int32)
        q_rows = slice(t * nq, (t + 1) * nq)
        for g in range(NA_HEADS // HEADS_PER_GROUP):
            cols = slice(g * GROUP_W, (g + 1) * GROUP_W)
            qg = q_ref[q_rows, cols]
            kg = k_ref[pl.ds(start, nk), cols]
            vg = v_ref[pl.ds(start, nk), cols]
            kcg = kc_ref[:, cols]
            vcg = vc_ref[:, cols]
            scores = []
            for hh in range(HEADS_PER_GROUP):
                qh = jnp.where(head_lanes[hh], qg, jnp.zeros_like(qg))
                scores.append([_dot_nt(qh, kg) + bias_ref[pattern, g * HEADS_PER_GROUP + hh], _dot_nt(qh, kcg)])
            acc = jnp.zeros((nq, GROUP_W), F32)
            for hh in range(HEADS_PER_GROUP):
                acc = jnp.where(head_lanes[hh], _softmax_pv(scores[hh], [vg, vcg]), acc)
            o_ref[q_rows, cols] = acc.astype(o_ref.dtype)


def _na_tile_geometry(rows):
    kr = min(NA_WIN_H, rows)
    key_rows = min(NA_TILE_ROWS + NA_WIN_H, rows)
    outside = 2 * NA_WIN_H - 1
    geo = []
    for r0 in range(0, rows, NA_TILE_ROWS):
        first = int(np.clip(r0 - NA_WIN_H // 2, 0, rows - key_rows))
        idx = np.full((NA_TILE_ROWS, key_rows), outside, np.int32)
        for i in range(NA_TILE_ROWS):
            r = r0 + i
            rs = int(np.clip(r - kr // 2, 0, rows - kr))
            assert first <= rs and rs + kr <= first + key_rows
            for j in range(rs - first, rs - first + kr):
                idx[i, j] = first + j - r + NA_WIN_H - 1
        geo.append(idx)
    return key_rows, geo


def _na_bias_table(rpb, rows):
    key_rows, geo = _na_tile_geometry(rows)
    assert all(np.array_equal(g, geo[1]) for g in geo[1:-1])
    col = np.arange(GRID_W)
    col_start = np.clip(col - NA_WIN_W // 2, 0, GRID_W - NA_WIN_W)
    col_in = (col[None, :] >= col_start[:, None]) & (col[None, :] < col_start[:, None] + NA_WIN_W)
    dc_idx = np.clip(col[None, :] - col[:, None] + NA_WIN_W - 1, 0, 2 * NA_WIN_W - 2)
    per_dr = jnp.where(col_in[None, None], rpb[:, :, dc_idx].astype(F32) * LOG2_E, -jnp.inf)
    per_dr = jnp.concatenate([per_dr, jnp.full_like(per_dr[:, :1], -jnp.inf)], axis=1)
    tabs = []
    for idx in (geo[0], geo[1], geo[-1]):
        blocks = per_dr[:, idx]
        tabs.append(jnp.transpose(blocks, (0, 1, 3, 2, 4)).reshape(
            NA_HEADS, NA_TILE_ROWS * GRID_W, key_rows * GRID_W))
    return key_rows, jnp.stack(tabs)


def _neighbourhood_attention(q, k, v, kc, vc, rpb):
    bsz, t, _ = q.shape
    lc = kc.shape[1]
    rows = t // GRID_W
    n_tiles = rows // NA_TILE_ROWS
    assert n_tiles >= 3 and n_tiles % NA_TILES_PER_STEP == 0
    key_rows, bias = _na_bias_table(rpb, rows)
    nq = NA_TILES_PER_STEP * NA_TILE_ROWS * GRID_W
    tile_spec = pl.BlockSpec((None, nq, NA_DIM), lambda b, i: (b, i, 0))
    seq_spec = pl.BlockSpec((None, t, NA_DIM), lambda b, i: (b, 0, 0))
    ctx_spec = pl.BlockSpec((None, lc, NA_DIM), lambda b, i: (b, 0, 0))
    return pl.pallas_call(
        functools.partial(_na_kernel, rows=rows, key_rows=key_rows),
        grid=(bsz, n_tiles // NA_TILES_PER_STEP),
        in_specs=[tile_spec, seq_spec, seq_spec, ctx_spec, ctx_spec, _const_spec(bias.shape)],
        out_specs=tile_spec,
        out_shape=jax.ShapeDtypeStruct((bsz, t, NA_DIM), BF16),
        compiler_params=_params(2),
        name="neighbourhood_attention",
    )(q, k, v, kc, vc, bias)


def _ctx_attn_kernel(q_ref, k_ref, v_ref, o_ref):
    nq = q_ref.shape[0]
    lane = lax.broadcasted_iota(jnp.int32, (nq, GROUP_W), 1)
    for g in range(NA_HEADS // HEADS_PER_GROUP):
        cols = slice(g * GROUP_W, (g + 1) * GROUP_W)
        qg = q_ref[:, cols]
        kg = k_ref[:, cols]
        vg = v_ref[:, cols]
        acc = jnp.zeros((nq, GROUP_W), F32)
        for hh in range(HEADS_PER_GROUP):
            head_lanes = (lane >= hh * NA_HEAD_DIM) & (lane < (hh + 1) * NA_HEAD_DIM)
            qh = jnp.where(head_lanes, qg, jnp.zeros_like(qg))
            o = _softmax_pv([_dot_nt(qh, kg)], [vg])
            acc = jnp.where(head_lanes, o, acc)
        o_ref[:, cols] = acc.astype(o_ref.dtype)


def _context_attention(q, k, v):
    bsz, lc, _ = q.shape
    spec = pl.BlockSpec((None, lc, NA_DIM), lambda b: (b, 0, 0))
    return pl.pallas_call(
        _ctx_attn_kernel,
        grid=(bsz,),
        in_specs=[spec, spec, spec],
        out_specs=spec,
        out_shape=jax.ShapeDtypeStruct((bsz, lc, NA_DIM), BF16),
        compiler_params=_params(1),
        name="context_attention",
    )(q, k, v)


def _hgrn_tables():
    L = HG_CHUNK
    t = np.arange(L)
    p_rows, masks = [], []
    for lev in range(HG_LEVELS):
        c = L >> lev
        mid = (t // c) * c + c // 2
        P = np.zeros((L, L), np.float32)
        for i in range(L):
            m = mid[i]
            if i >= m:
                P[i, m + 1:i + 1] = 1.0
            else:
                P[i, i + 1:m + 1] = 1.0
        p_rows.append(P)
        same = (t[:, None] // c) == (t[None, :] // c)
        masks.append((same & (t[:, None] >= mid[:, None]) & (t[None, :] < mid[None, :])).astype(np.float32))
    incl = np.tril(np.ones((L, L), np.float32))
    fine = p_rows[HG_MATMUL_LEVEL:]
    pf = np.concatenate([incl] + fine, axis=0)
    flip = lambda a: a[..., ::-1, ::-1]
    pb = np.concatenate([flip(incl)] + [flip(p) for p in fine], axis=0)
    side_by_side = lambda ms: np.stack([np.concatenate([m, m], axis=1) for m in ms])
    mf = side_by_side(masks)
    mb = side_by_side([flip(m) for m in masks])
    return pf, pb, mf, mb


def _hgrn_gates(z, log_lb, log1m_lb, one_m_lb):
    e = jnp.exp(-jnp.abs(z))
    inv = 1.0 / (1.0 + e)
    key = one_m_lb * jnp.where(z >= 0, e * inv, inv)
    c = log1m_lb + (jnp.minimum(z, 0.0) - jnp.log(1.0 + e))
    log_f = jnp.maximum(log_lb, c) + jnp.log(1.0 + jnp.exp(-jnp.abs(log_lb - c)))
    return key, log_f


def _hgrn_group(q_ref, v_ref, z_ref, first_chunk, G, lbp, p_ref, m_ref, st_ref, reverse, emit):
    L, dk = HG_CHUNK, HG_HEAD_DIM
    base = pl.multiple_of(first_chunk * L, G * L)
    group_rows = pl.ds(base, G * L)
    chunk = lambda a, j: a[j * L:(j + 1) * L]
    wide = lambda a: jnp.concatenate([chunk(a, j) for j in range(G)], axis=1)

    q, v, z = q_ref[group_rows, :], v_ref[group_rows, :], z_ref[group_rows, :]
    key, log_f = _hgrn_gates(z, lbp[0], lbp[1], lbp[2])
    log2_f = log_f * LOG2_E
    g_hi = log2_f.astype(BF16)
    g_lo = (log2_f - g_hi.astype(F32)).astype(BF16)
    p = p_ref[...]
    x = _dot(p, wide(g_hi)) + _dot(p, wide(g_lo))
    cum = x[:L]

    def level_exponent(c):
        pieces = []
        for i in range(L // c):
            mid = i * c + c // 2 - (1 if reverse else 0)
            ref = cum[mid:mid + 1]
            blk = cum[i * c:(i + 1) * c]
            if c >= 2 * SUBLANES:
                first_half, second_half = blk[:c // 2], blk[c // 2:]
                pieces += ([first_half - ref, ref - second_half] if reverse
                           else [ref - first_half, second_half - ref])
            else:
                pieces.append(-jnp.abs(blk - ref))
        return jnp.concatenate(pieces, axis=0)

    last_row = 0 if reverse else L - 1
    exponents = ([level_exponent(L >> lev) for lev in range(HG_MATMUL_LEVEL)]
                 + [x[(1 + i) * L:(2 + i) * L] for i in range(HG_LEVELS - HG_MATMUL_LEVEL)]
                 + [cum, cum[last_row:last_row + 1] - cum])
    decay = [jnp.exp2(e) for e in exponents]
    decay16 = [d.astype(BF16) for d in decay]
    dec16 = lambda blk, j: decay16[blk][:, j * dk:(j + 1) * dk]
    q16, key16 = q.astype(BF16), key.astype(BF16)
    qs = [chunk(q16, j) for j in range(G)]
    ks = [chunk(key16, j) for j in range(G)]
    vs = [chunk(v, j) for j in range(G)]
    qk = jnp.sum(q * key, axis=-1, keepdims=True)

    zero = jnp.zeros((L, dk), BF16)
    block_diag = lambda a, b: jnp.concatenate([jnp.concatenate([a, zero], axis=1),
                                               jnp.concatenate([zero, b], axis=1)], axis=0)
    row = lax.broadcasted_iota(jnp.int32, (L, 2 * L), 0)
    col = lax.broadcasted_iota(jnp.int32, (L, 2 * L), 1)
    pairs = range(0, G, 2)
    a = [jnp.where(col == row, chunk(qk, j), jnp.where(col == row + L, chunk(qk, j + 1), 0.0)) for j in pairs]
    for lev in range(HG_LEVELS):
        for i, j in enumerate(pairs):
            e0, e1 = dec16(lev, j), dec16(lev, j + 1)
            q_pair = jnp.concatenate([qs[j] * e0, qs[j + 1] * e1], axis=1)
            a[i] = a[i] + m_ref[lev] * _dot_nt(q_pair, block_diag(ks[j] * e0, ks[j + 1] * e1))
    o_local = []
    for i, j in enumerate(pairs):
        o_pair = _dot(a[i].astype(BF16), block_diag(vs[j], vs[j + 1]))
        o_local += [o_pair[:, :dk], o_pair[:, dk:]]
    q_in = [qs[j] * dec16(HG_LEVELS, j) for j in range(G)]
    st_add = [_dot_tn(vs[j], ks[j] * dec16(HG_LEVELS + 1, j)) for j in range(G)]
    e_all = [decay[HG_LEVELS][last_row:last_row + 1, j * dk:(j + 1) * dk] for j in range(G)]

    st = st_ref[...]
    for j in (reversed(range(G)) if reverse else range(G)):
        emit(pl.ds(base + j * L, L), o_local[j] + _dot_nt(q_in[j], st.astype(BF16)))
        st = st * e_all[j] + st_add[j]
    st_ref[...] = st


def _hgrn_kernel(q_ref, v_ref, zf_ref, zb_ref, g_ref, qc_ref, vc_ref, zfc_ref, zbc_ref, gc_ref,
                 lb_ref, ng_ref, pf_ref, pb_ref, mf_ref, mb_ref, *rest, ctx_out):
    if ctx_out:
        o_ref, oc_ref, st_ref, of_ref, ofc_ref = rest
    else:
        o_ref, st_ref, of_ref = rest
        oc_ref = ofc_ref = None
    norm_g = ng_ref[...]

    def readout(o, g):
        o = o * lax.rsqrt(jnp.mean(o * o, axis=-1, keepdims=True) + RMS_EPS) * norm_g
        return (o * _silu(g)).astype(BF16)

    def store_to(dst_ref):
        def emit(rows, o):
            dst_ref[rows, :] = o
        return emit

    def readout_to(dst_ref, fwd_ref, gate_ref):
        def emit(rows, o):
            dst_ref[rows, :] = readout(fwd_ref[rows, :] + o, gate_ref[rows, :])
        return emit

    def drop(rows, o):
        pass

    def scan(refs, lbp, p_ref, m_ref, reverse, emit):
        n_chunks = refs[0].shape[0] // HG_CHUNK
        group = min(HG_GROUP, n_chunks)
        n_groups = n_chunks // group

        def body(i, carry):
            first_chunk = ((n_groups - 1 - i) if reverse else i) * group
            _hgrn_group(*refs, first_chunk, group, lbp, p_ref, m_ref, st_ref, reverse, emit)
            return carry
        lax.fori_loop(0, n_groups, body, 0)

    lbf = (lb_ref[0, 0], lb_ref[0, 1], lb_ref[0, 2])
    st_ref[...] = jnp.zeros_like(st_ref)
    scan((qc_ref, vc_ref, zfc_ref), lbf, pf_ref, mf_ref, False, store_to(ofc_ref) if ctx_out else drop)
    scan((q_ref, v_ref, zf_ref), lbf, pf_ref, mf_ref, False, store_to(of_ref))

    lbb = (lb_ref[1, 0], lb_ref[1, 1], lb_ref[1, 2])
    st_ref[...] = jnp.zeros_like(st_ref)
    scan((qc_ref, vc_ref, zbc_ref), lbb, pb_ref, mb_ref, True,
         readout_to(oc_ref, ofc_ref, gc_ref) if ctx_out else drop)
    scan((q_ref, v_ref, zb_ref), lbb, pb_ref, mb_ref, True, readout_to(o_ref, of_ref, g_ref))


def _hgrn2(main, ctx, lb, norm_g, ctx_out):
    q = main[0]
    bsz, t, _ = q.shape
    lc = ctx[0].shape[1]
    pf, pb, mf, mb = _hgrn_tables()
    lb = lb.astype(F32)
    lb_params = jnp.stack([jnp.log(lb), jnp.log1p(-lb), 1.0 - lb], axis=1)
    lb_params = jnp.transpose(lb_params, (2, 0, 1, 3))[:, :, :, None, :]
    seq_spec = pl.BlockSpec((None, t, HG_HEAD_DIM), lambda b, h: (b, 0, h))
    ctx_spec = pl.BlockSpec((None, lc, HG_HEAD_DIM), lambda b, h: (b, 0, h))
    in_specs = ([seq_spec] * 5 + [ctx_spec] * 5 + [
        pl.BlockSpec((None, 2, 3, 1, HG_HEAD_DIM), lambda b, h: (h, 0, 0, 0, 0)),
        pl.BlockSpec((None, 1, HG_HEAD_DIM), lambda b, h: (h, 0, 0)),
        _const_spec(pf.shape), _const_spec(pb.shape), _const_spec(mf.shape), _const_spec(mb.shape)])
    out_specs = [seq_spec]
    out_shape = [jax.ShapeDtypeStruct((bsz, t, HG_DIM), BF16)]
    scratch = [pltpu.VMEM((HG_HEAD_DIM, HG_HEAD_DIM), F32), pltpu.VMEM((t, HG_HEAD_DIM), F32)]
    if ctx_out:
        out_specs.append(ctx_spec)
        out_shape.append(jax.ShapeDtypeStruct((bsz, lc, HG_DIM), BF16))
        scratch.append(pltpu.VMEM((lc, HG_HEAD_DIM), F32))
    outs = pl.pallas_call(
        functools.partial(_hgrn_kernel, ctx_out=ctx_out),
        grid=(bsz, HG_HEADS),
        in_specs=in_specs,
        out_specs=out_specs,
        out_shape=out_shape,
        scratch_shapes=scratch,
        compiler_params=_params(2),
        name="hgrn2",
    )(*main, *ctx, lb_params, norm_g.astype(F32)[:, None, :],
      jnp.asarray(pf, BF16), jnp.asarray(pb, BF16), jnp.asarray(mf), jnp.asarray(mb))
    return (outs[0], outs[1]) if ctx_out else (outs[0], None)


LANE_BLOCK = 128
CONV_ROW_BLOCK = 256


def _conv_kernel(prev_ref, cur_ref, next_ref, w_ref, b_ref, g_ref, beta_ref, o_ref, win_ref, acc_ref,
                 shift_ref):
    i = pl.program_id(1)
    tm = cur_ref.shape[0]
    win_ref[0:CONV_HALO, :] = jnp.where(i > 0, prev_ref[...], 0.0)
    win_ref[CONV_HALO:CONV_HALO + tm, :] = cur_ref[...]
    win_ref[CONV_HALO + tm:, :] = jnp.where(i < pl.num_programs(1) - 1, next_ref[...], 0.0)
    first = CONV_HALO - CONV_WIDTH // 2
    span = tm + 2 * CONV_HALO - SUBLANES
    def lane_block(c, carry):
        cols = pl.ds(pl.multiple_of(c * LANE_BLOCK, LANE_BLOCK), LANE_BLOCK)
        for phase in range(SUBLANES):
            shift_ref[phase] = win_ref[phase:phase + span, cols]
        for r0 in range(0, tm, CONV_ROW_BLOCK):
            acc = jnp.zeros((CONV_ROW_BLOCK, LANE_BLOCK), F32)
            for k in range(CONV_WIDTH):
                phase = (first + k) % SUBLANES
                off = first + k - phase + r0
                acc = acc + shift_ref[phase, off:off + CONV_ROW_BLOCK, :] * w_ref[k, :, cols]
            acc_ref[r0:r0 + CONV_ROW_BLOCK, cols] = acc + b_ref[:, cols]
        return carry

    lax.fori_loop(0, D_MODEL // LANE_BLOCK, lane_block, 0)
    y = _layer_norm(acc_ref[...], g_ref[...], beta_ref[...])
    o_ref[...] = _silu(y).astype(o_ref.dtype)


def _conv_module_mid(u, w_dw, b_dw, ln_g, ln_b, tm):
    bsz, t, d = u.shape
    hb = tm // CONV_HALO
    n_hb = t // CONV_HALO
    vec = lambda a: a.astype(F32).reshape(1, d)
    w_taps = w_dw.astype(F32).reshape(CONV_WIDTH, 1, d)
    return pl.pallas_call(
        _conv_kernel,
        grid=(bsz, t // tm),
        in_specs=[
            pl.BlockSpec((None, CONV_HALO, d), lambda b, i: (b, jnp.maximum(i * hb - 1, 0), 0)),
            pl.BlockSpec((None, tm, d), lambda b, i: (b, i, 0)),
            pl.BlockSpec((None, CONV_HALO, d), lambda b, i: (b, jnp.minimum((i + 1) * hb, n_hb - 1), 0)),
            _const_spec((CONV_WIDTH, 1, d)), _const_spec((1, d)), _const_spec((1, d)), _const_spec((1, d)),
        ],
        out_specs=pl.BlockSpec((None, tm, d), lambda b, i: (b, i, 0)),
        out_shape=jax.ShapeDtypeStruct((bsz, t, d), BF16),
        scratch_shapes=[pltpu.VMEM((tm + 2 * CONV_HALO, d), F32), pltpu.VMEM((tm, d), F32),
                        pltpu.VMEM((SUBLANES, tm + 2 * CONV_HALO - SUBLANES, LANE_BLOCK), F32)],
        compiler_params=_params(2),
        name="conv_module",
    )(u, u, u, w_taps, vec(b_dw), vec(ln_g), vec(ln_b))


def _ffn_kernel(*refs, n_act):
    x_ref = refs[0]
    act_refs = refs[1:1 + n_act]
    wo_refs = refs[1 + n_act:1 + 2 * n_act]
    (bo_ref, g1_ref, sh2_ref, sc2_ref, g2_ref, l1g_ref, l1b_ref, l2g_ref, l2b_ref,
     w1_ref, w3_ref, w2_ref, o_ref) = refs[1 + 2 * n_act:]
    sub = x_ref.shape[0] // FFN_ROW_SPLIT
    rows = [slice(r * sub, (r + 1) * sub) for r in range(FFN_ROW_SPLIT)]

    def out_proj(r):
        y = bo_ref[...]
        for a_ref, w_ref in zip(act_refs, wo_refs):
            y = y + _dot(a_ref[rows[r], :], w_ref[...])
        return y

    def norm1(r, y):
        x1 = _layer_norm(ALPHA * x_ref[rows[r], :] + g1_ref[...] * y, l1g_ref[...], l1b_ref[...])
        return x1, (x1 * (1.0 + sc2_ref[...]) + sh2_ref[...]).astype(BF16)

    def up(r, h):
        return _dot(h, w1_ref[...]), _dot(h, w3_ref[...])

    def gate(r, h1, h3):
        return (_silu(h1) * h3).astype(BF16)

    def down(r, u):
        return _dot(u, w2_ref[...])

    def norm2(r, x1, f):
        o_ref[rows[r], :] = _layer_norm(ALPHA * x1 + g2_ref[...] * f, l2g_ref[...], l2b_ref[...])

    stages = 6
    state = [None] * FFN_ROW_SPLIT
    for step in range(stages + FFN_ROW_SPLIT - 1):
        for r in range(FFN_ROW_SPLIT):
            stage = step - r
            if stage == 0:
                state[r] = out_proj(r)
            elif stage == 1:
                state[r] = norm1(r, state[r])
            elif stage == 2:
                x1, h = state[r]
                state[r] = (x1, up(r, h))
            elif stage == 3:
                x1, (h1, h3) = state[r]
                state[r] = (x1, gate(r, h1, h3))
            elif stage == 4:
                x1, u = state[r]
                state[r] = (x1, down(r, u))
            elif stage == 5:
                norm2(r, *state[r])


def _out_ffn(x, acts, w_outs, b_out, mods, ln, w1, w3, w2, tm):
    bsz, t, d = x.shape
    n_act = len(acts)
    tok = lambda width: pl.BlockSpec((None, tm, width), lambda b, i: (b, i, 0))
    vec = lambda a: a.astype(F32).reshape(1, d)
    g1, sh2, sc2, g2 = mods
    in_specs = ([tok(d)] + [tok(a.shape[-1]) for a in acts] + [_const_spec(w.shape) for w in w_outs]
                + [_const_spec((1, d))] + [_mod_spec(m) for m in mods] + [_const_spec((1, d))] * 4
                + [_const_spec(w1.shape), _const_spec(w3.shape), _const_spec(w2.shape)])
    return pl.pallas_call(
        functools.partial(_ffn_kernel, n_act=n_act),
        grid=(bsz, t // tm),
        in_specs=in_specs,
        out_specs=tok(d),
        out_shape=jax.ShapeDtypeStruct((bsz, t, d), F32),
        compiler_params=_params(2),
        name="out_ffn",
    )(x, *acts, *w_outs, vec(b_out), g1, sh2, sc2, g2, *[vec(a) for a in ln], w1, w3, w2)


def kernel(x, c, ctx, c_ctx, w_mod, b_mod, ln1_g, ln1_b, ln2_g, ln2_b, w_in_even, w_out_even, na_rpb, hg_lb,
           hg_norm_g, cv_w1, cv_b1, cv_wdw, cv_bdw, cv_ln_g, cv_ln_b, cv_w2, cv_b2, ffn_w1, ffn_w3, ffn_w2):
    bsz, t, d = x.shape
    lc = ctx.shape[1]
    depth = w_mod.shape[0]
    last_even = depth - 1 if (depth - 1) % 2 == 0 else depth - 2
    tm = min(512, t)
    tmc = min(512, lc)

    p_lb = jax.nn.softmax(hg_lb.astype(F32), axis=1)
    hg_lower = jnp.maximum(jnp.cumsum(p_lb, axis=1) - p_lb[:, :1], 0.0)

    c_all = jnp.zeros((MOD_ROWS, d), F32).at[:bsz].set(c).at[bsz].set(c_ctx)
    mod_all = _modulation(c_all, w_mod, b_mod)

    xc = ctx
    for l in range(depth):
        ctx_used = l <= last_even
        ctx_out = l < last_even
        mod = [mod_all[l, :bsz, k * d:(k + 1) * d].reshape(bsz, 1, d) for k in range(6)]
        mod_c = [mod_all[l, bsz:bsz + 1, k * d:(k + 1) * d].reshape(1, 1, d) for k in range(6)]
        ln = (ln1_g[l], ln1_b[l], ln2_g[l], ln2_b[l])
        w1, w3, w2 = ffn_w1[l].astype(BF16), ffn_w3[l].astype(BF16), ffn_w2[l].astype(BF16)
        j = l // 2
        if l % 2 == 0:
            w_in = w_in_even[j].astype(BF16)
            w_out = w_out_even[j].astype(BF16)
            w_outs = [w_out[:NA_DIM], w_out[NA_DIM:]]
            b_out = jnp.zeros((d,), F32)
            qa, ka, va, qb, ib, zf, zb, gb = _inproj_even(x, mod[0], mod[1], w_in, tm)
            qa_c, ka_c, va_c, qb_c, ib_c, zf_c, zb_c, gb_c = _inproj_even(xc, mod_c[0], mod_c[1], w_in, tmc)
            o_na = _neighbourhood_attention(qa, ka, va, ka_c, va_c, na_rpb[j])
            o_hg, o_hg_c = _hgrn2((qb, ib, zf, zb, gb), (qb_c, ib_c, zf_c, zb_c, gb_c), hg_lower[:, j],
                                  hg_norm_g[j], ctx_out)
            acts = [o_na, o_hg]
            if ctx_out:
                acts_c = [_context_attention(qa_c, ka_c, va_c), o_hg_c]
        else:
            w_outs = [cv_w2[j].astype(BF16)]
            b_out = cv_b2[j]
            conv = (cv_wdw[j], cv_bdw[j], cv_ln_g[j], cv_ln_b[j])
            w_cv1 = cv_w1[j].astype(BF16)
            b_cv1 = cv_b1[j].astype(F32).reshape(1, 2 * d)
            acts = [_conv_module_mid(_inproj_odd(x, mod[0], mod[1], w_cv1, b_cv1, tm), *conv, tm)]
            if ctx_out:
                acts_c = [_conv_module_mid(_inproj_odd(xc, mod_c[0], mod_c[1], w_cv1, b_cv1, tmc), *conv, tmc)]
        x = _out_ffn(x, acts, w_outs, b_out, mod[2:], ln, w1, w3, w2, tm)
        if ctx_out:
            xc = _out_ffn(xc, acts_c, w_outs, b_out, mod_c[2:], ln, w1, w3, w2, tmc)
    return x
```

```python
import functools

import numpy as np
import jax
import jax.numpy as jnp
from jax import lax
from jax.experimental import pallas as pl
from jax.experimental.pallas import tpu as pltpu

F32 = jnp.float32
BF16 = jnp.bfloat16

D_MODEL = 1024
DEPTH = 4
GRID_W = 64
NA_HEADS = 8
NA_HEAD_DIM = 64
NA_DIM = NA_HEADS * NA_HEAD_DIM
NA_WIN_H = 8
NA_WIN_W = 16
HG_HEADS = 4
HG_HEAD_DIM = 128
HG_DIM = HG_HEADS * HG_HEAD_DIM
HG_CHUNK = 64
HG_LEVELS = 6
SUBLANES = 8
HG_MATMUL_LEVEL = 4
LOG2_E = 1.4426950408889634
CONV_WIDTH = 31
CONV_HALO = 16
D_FF = 2816
ALPHA = (2 * DEPTH) ** 0.25
LN_EPS = 1e-5
FFN_ROW_SPLIT = 2
RMS_EPS = 1e-6

VMEM_LIMIT_BYTES = 56 * 1024 * 1024
MOD_ROWS = 24


def _params(n_grid, vmem=VMEM_LIMIT_BYTES):
    return pltpu.CompilerParams(dimension_semantics=("arbitrary",) * n_grid, vmem_limit_bytes=vmem)


def _const_spec(shape):
    nd = len(shape)
    return pl.BlockSpec(shape, lambda *_: (0,) * nd, pipeline_mode=pl.Buffered(1))


def _layer_norm(x, g, b):
    mu = jnp.mean(x, axis=-1, keepdims=True)
    xc = x - mu
    var = jnp.mean(xc * xc, axis=-1, keepdims=True)
    return xc * lax.rsqrt(var + LN_EPS) * g + b


def _silu(x):
    return x * jax.nn.sigmoid(x)


def _dot(a, b):
    return jnp.dot(a, b, preferred_element_type=F32)


def _dot_nt(a, b):
    return lax.dot_general(a, b, (((1,), (1,)), ((), ())), preferred_element_type=F32)


def _dot_tn(a, b):
    return lax.dot_general(a, b, (((0,), (0,)), ((), ())), preferred_element_type=F32)


def _mod_kernel(c_ref, w_ref, b_ref, o_ref):
    s = _silu(c_ref[...]).astype(BF16)
    o_ref[...] = _dot(s, w_ref[...].astype(BF16)) + b_ref[...]


def _modulation(c_all, w_mod, b_mod):
    depth, d, n = w_mod.shape
    tn = 1536
    return pl.pallas_call(
        _mod_kernel,
        grid=(depth, n // tn),
        in_specs=[
            pl.BlockSpec((MOD_ROWS, d), lambda l, j: (0, 0)),
            pl.BlockSpec((None, d, tn), lambda l, j: (l, 0, j)),
            pl.BlockSpec((None, 1, tn), lambda l, j: (l, 0, j)),
        ],
        out_specs=pl.BlockSpec((None, MOD_ROWS, tn), lambda l, j: (l, 0, j)),
        out_shape=jax.ShapeDtypeStruct((depth, MOD_ROWS, n), F32),
        compiler_params=_params(2),
        name="modulation",
    )(c_all, w_mod, b_mod.reshape(depth, 1, n))


def _mod_spec(mod):
    per_batch = 1 if mod.shape[0] > 1 else 0
    return pl.BlockSpec((None, 1, D_MODEL), lambda b, i: (b * per_batch, 0, 0))


def _inproj_even_kernel(x_ref, sh_ref, sc_ref, w_ref, qa_ref, ka_ref, va_ref, qb_ref, ib_ref, zf_ref,
                        zb_ref, gb_ref):
    h = (x_ref[...] * (1.0 + sc_ref[...]) + sh_ref[...]).astype(BF16)
    outs = (qa_ref, ka_ref, va_ref, qb_ref, ib_ref, zf_ref, zb_ref, gb_ref)
    for s, o_ref in enumerate(outs):
        y = _dot(h, w_ref[:, s * NA_DIM:(s + 1) * NA_DIM])
        if s == 0:
            y = y * (NA_HEAD_DIM ** -0.5 * LOG2_E)
        o_ref[...] = y.astype(o_ref.dtype)


def _inproj_even(x, sh, sc, w_bf16, tm):
    bsz, t, d = x.shape
    tok = pl.BlockSpec((None, tm, d), lambda b, i: (b, i, 0))
    out_spec = pl.BlockSpec((None, tm, NA_DIM), lambda b, i: (b, i, 0))
    dtypes = (BF16, BF16, BF16, F32, BF16, F32, F32, F32)
    return pl.pallas_call(
        _inproj_even_kernel,
        grid=(bsz, t // tm),
        in_specs=[tok, _mod_spec(sh), _mod_spec(sc), _const_spec(w_bf16.shape)],
        out_specs=[out_spec] * 8,
        out_shape=[jax.ShapeDtypeStruct((bsz, t, NA_DIM), dt) for dt in dtypes],
        compiler_params=_params(2),
        name="inproj_even",
    )(x, sh, sc, w_bf16)


def _inproj_odd_kernel(x_ref, sh_ref, sc_ref, w_ref, b_ref, u_ref):
    h = (x_ref[...] * (1.0 + sc_ref[...]) + sh_ref[...]).astype(BF16)
    a = _dot(h, w_ref[:, :D_MODEL]) + b_ref[:, :D_MODEL]
    g = _dot(h, w_ref[:, D_MODEL:]) + b_ref[:, D_MODEL:]
    u_ref[...] = a * jax.nn.sigmoid(g)


def _inproj_odd(x, sh, sc, w_bf16, bias, tm):
    bsz, t, d = x.shape
    tok = pl.BlockSpec((None, tm, d), lambda b, i: (b, i, 0))
    return pl.pallas_call(
        _inproj_odd_kernel,
        grid=(bsz, t // tm),
        in_specs=[tok, _mod_spec(sh), _mod_spec(sc), _const_spec(w_bf16.shape), _const_spec(bias.shape)],
        out_specs=tok,
        out_shape=jax.ShapeDtypeStruct((bsz, t, d), F32),
        compiler_params=_params(2),
        name="inproj_odd",
    )(x, sh, sc, w_bf16, bias)


HEADS_PER_GROUP = 4
GROUP_W = HEADS_PER_GROUP * NA_HEAD_DIM
NA_TILE_ROWS = 4
NA_TILES_PER_STEP = 2


def _softmax_pv(s_parts, v_parts):
    m = s_parts[0].max(axis=-1, keepdims=True)
    for s in s_parts[1:]:
        m = jnp.maximum(m, s.max(axis=-1, keepdims=True))
    ps = [jnp.exp2(s - m) for s in s_parts]
    l = ps[0].sum(axis=-1, keepdims=True)
    for p in ps[1:]:
        l = l + p.sum(axis=-1, keepdims=True)
    o = _dot(ps[0].astype(BF16), v_parts[0])
    for p, v in zip(ps[1:], v_parts[1:]):
        o = o + _dot(p.astype(BF16), v)
    return o * (1.0 / l)


def _na_kernel(q_ref, k_ref, v_ref, kc_ref, vc_ref, bias_ref, o_ref, *, rows, key_rows):
    n_tiles = rows // NA_TILE_ROWS
    nq = NA_TILE_ROWS * GRID_W
    nk = key_rows * GRID_W
    lane = lax.broadcasted_iota(jnp.int32, (nq, GROUP_W), 1)
    head_lanes = [(lane >= hh * NA_HEAD_DIM) & (lane < (hh + 1) * NA_HEAD_DIM) for hh in range(HEADS_PER_GROUP)]
    for t in range(NA_TILES_PER_STEP):
        tile = pl.program_id(1) * NA_TILES_PER_STEP + t
        first_row = jnp.clip(tile * NA_TILE_ROWS - NA_WIN_H // 2, 0, rows - key_rows)
        start = pl.multiple_of(first_row * GRID_W, GRID_W)
        pattern = jnp.minimum(tile, 1) + (tile == n_tiles - 1).astype(jnp.int32)
        q_rows = slice(t * nq, (t + 1) * nq)
        for g in range(NA_HEADS // HEADS_PER_GROUP):
            cols = slice(g * GROUP_W, (g + 1) * GROUP_W)
            qg = q_ref[q_rows, cols]
            kg = k_ref[pl.ds(start, nk), cols]
            vg = v_ref[pl.ds(start, nk), cols]
            kcg = kc_ref[:, cols]
            vcg = vc_ref[:, cols]
            scores = []
            for hh in range(HEADS_PER_GROUP):
                qh = jnp.where(head_lanes[hh], qg, jnp.zeros_like(qg))
                scores.append([_dot_nt(qh, kg) + bias_ref[pattern, g * HEADS_PER_GROUP + hh], _dot_nt(qh, kcg)])
            acc = jnp.zeros((nq, GROUP_W), F32)
            for hh in range(HEADS_PER_GROUP):
                acc = jnp.where(head_lanes[hh], _softmax_pv(scores[hh], [vg, vcg]), acc)
            o_ref[q_rows, cols] = acc.astype(o_ref.dtype)


def _na_tile_geometry(rows):
    kr = min(NA_WIN_H, rows)
    key_rows = min(NA_TILE_ROWS + NA_WIN_H, rows)
    outside = 2 * NA_WIN_H - 1
    geo = []
    for r0 in range(0, rows, NA_TILE_ROWS):
        first = int(np.clip(r0 - NA_WIN_H // 2, 0, rows - key_rows))
        idx = np.full((NA_TILE_ROWS, key_rows), outside, np.int32)
        for i in range(NA_TILE_ROWS):
            r = r0 + i
            rs = int(np.clip(r - kr // 2, 0, rows - kr))
            assert first <= rs and rs + kr <= first + key_rows
            for j in range(rs - first, rs - first + kr):
                idx[i, j] = first + j - r + NA_WIN_H - 1
        geo.append(idx)
    return key_rows, geo


def _na_bias_table(rpb, rows):
    key_rows, geo = _na_tile_geometry(rows)
    assert all(np.array_equal(g, geo[1]) for g in geo[1:-1])
    col = np.arange(GRID_W)
    col_start = np.clip(col - NA_WIN_W // 2, 0, GRID_W - NA_WIN_W)
    col_in = (col[None, :] >= col_start[:, None]) & (col[None, :] < col_start[:, None] + NA_WIN_W)
    dc_idx = np.clip(col[None, :] - col[:, None] + NA_WIN_W - 1, 0, 2 * NA_WIN_W - 2)
    per_dr = jnp.where(col_in[None, None], rpb[:, :, dc_idx].astype(F32) * LOG2_E, -jnp.inf)
    per_dr = jnp.concatenate([per_dr, jnp.full_like(per_dr[:, :1], -jnp.inf)], axis=1)
    tabs = []
    for idx in (geo[0], geo[1], geo[-1]):
        blocks = per_dr[:, idx]
        tabs.append(jnp.transpose(blocks, (0, 1, 3, 2, 4)).reshape(
            NA_HEADS, NA_TILE_ROWS * GRID_W, key_rows * GRID_W))
    return key_rows, jnp.stack(tabs)


def _neighbourhood_attention(q, k, v, kc, vc, rpb):
    bsz, t, _ = q.shape
    lc = kc.shape[1]
    rows = t // GRID_W
    n_tiles = rows // NA_TILE_ROWS
    assert n_tiles >= 3 and n_tiles % NA_TILES_PER_STEP == 0
    key_rows, bias = _na_bias_table(rpb, rows)
    nq = NA_TILES_PER_STEP * NA_TILE_ROWS * GRID_W
    tile_spec = pl.BlockSpec((None, nq, NA_DIM), lambda b, i: (b, i, 0))
    seq_spec = pl.BlockSpec((None, t, NA_DIM), lambda b, i: (b, 0, 0))
    ctx_spec = pl.BlockSpec((None, lc, NA_DIM), lambda b, i: (b, 0, 0))
    return pl.pallas_call(
        functools.partial(_na_kernel, rows=rows, key_rows=key_rows),
        grid=(bsz, n_tiles // NA_TILES_PER_STEP),
        in_specs=[tile_spec, seq_spec, seq_spec, ctx_spec, ctx_spec, _const_spec(bias.shape)],
        out_specs=tile_spec,
        out_shape=jax.ShapeDtypeStruct((bsz, t, NA_DIM), BF16),
        compiler_params=_params(2),
        name="neighbourhood_attention",
    )(q, k, v, kc, vc, bias)


def _ctx_attn_kernel(q_ref, k_ref, v_ref, o_ref):
    nq = q_ref.shape[0]
    lane = lax.broadcasted_iota(jnp.int32, (nq, GROUP_W), 1)
    for g in range(NA_HEADS // HEADS_PER_GROUP):
        cols = slice(g * GROUP_W, (g + 1) * GROUP_W)
        qg = q_ref[:, cols]
        kg = k_ref[:, cols]
        vg = v_ref[:, cols]
        acc = jnp.zeros((nq, GROUP_W), F32)
        for hh in range(HEADS_PER_GROUP):
            head_lanes = (lane >= hh * NA_HEAD_DIM) & (lane < (hh + 1) * NA_HEAD_DIM)
            qh = jnp.where(head_lanes, qg, jnp.zeros_like(qg))
            o = _softmax_pv([_dot_nt(qh, kg)], [vg])
            acc = jnp.where(head_lanes, o, acc)
        o_ref[:, cols] = acc.astype(o_ref.dtype)


def _context_attention(q, k, v):
    bsz, lc, _ = q.shape
    spec = pl.BlockSpec((None, lc, NA_DIM), lambda b: (b, 0, 0))
    return pl.pallas_call(
        _ctx_attn_kernel,
        grid=(bsz,),
        in_specs=[spec, spec, spec],
        out_specs=spec,
        out_shape=jax.ShapeDtypeStruct((bsz, lc, NA_DIM), BF16),
        compiler_params=_params(1),
        name="context_attention",
    )(q, k, v)


def _hgrn_tables():
    L = HG_CHUNK
    t = np.arange(L)
    p_rows, masks = [], []
    for lev in range(HG_LEVELS):
        c = L >> lev
        mid = (t // c) * c + c // 2
        P = np.zeros((L, L), np.float32)
        for i in range(L):
            m = mid[i]
            if i >= m:
                P[i, m + 1:i + 1] = 1.0
            else:
                P[i, i + 1:m + 1] = 1.0
        p_rows.append(P)
        same = (t[:, None] // c) == (t[None, :] // c)
        masks.append((same & (t[:, None] >= mid[:, None]) & (t[None, :] < mid[None, :])).astype(np.float32))
    incl = np.tril(np.ones((L, L), np.float32))
    fine = p_rows[HG_MATMUL_LEVEL:]
    pf = np.concatenate([incl] + fine, axis=0)
    flip = lambda a: a[..., ::-1, ::-1]
    pb = np.concatenate([flip(incl)] + [flip(p) for p in fine], axis=0)
    side_by_side = lambda ms: np.stack([np.concatenate([m, m], axis=1) for m in ms])
    mf = side_by_side(masks)
    mb = side_by_side([flip(m) for m in masks])
    return pf, pb, mf, mb


def _hgrn_gates(z, log_lb, log1m_lb, one_m_lb):
    e = jnp.exp(-jnp.abs(z))
    inv = 1.0 / (1.0 + e)
    key = one_m_lb * jnp.where(z >= 0, e * inv, inv)
    c = log1m_lb + (jnp.minimum(z, 0.0) - jnp.log(1.0 + e))
    log_f = jnp.maximum(log_lb, c) + jnp.log(1.0 + jnp.exp(-jnp.abs(log_lb - c)))
    return key, log_f


def _hgrn_direction(parts, lbp, p_ref, m_ref, reverse):
    L, dk = HG_CHUNK, HG_HEAD_DIM
    owner = [(part[3], c) for part in parts for c in range(part[0].shape[0] // L)]
    G = len(owner)
    chunk = lambda a, j: a[j * L:(j + 1) * L]
    wide = lambda a: jnp.concatenate([chunk(a, j) for j in range(G)], axis=1)

    q, v, z = (jnp.concatenate([part[i][...] for part in parts], axis=0) for i in range(3))
    key, log_f = _hgrn_gates(z, lbp[0], lbp[1], lbp[2])
    log2_f = log_f * LOG2_E
    g_hi = log2_f.astype(BF16)
    g_lo = (log2_f - g_hi.astype(F32)).astype(BF16)
    p = p_ref[...]
    x = _dot(p, wide(g_hi)) + _dot(p, wide(g_lo))
    cum = x[:L]

    def level_exponent(c):
        pieces = []
        for i in range(L // c):
            mid = i * c + c // 2 - (1 if reverse else 0)
            ref = cum[mid:mid + 1]
            blk = cum[i * c:(i + 1) * c]
            if c >= 2 * SUBLANES:
                first_half, second_half = blk[:c // 2], blk[c // 2:]
                pieces += ([first_half - ref, ref - second_half] if reverse
                           else [ref - first_half, second_half - ref])
            else:
                pieces.append(-jnp.abs(blk - ref))
        return jnp.concatenate(pieces, axis=0)

    last_row = 0 if reverse else L - 1
    exponents = ([level_exponent(L >> lev) for lev in range(HG_MATMUL_LEVEL)]
                 + [x[(1 + i) * L:(2 + i) * L] for i in range(HG_LEVELS - HG_MATMUL_LEVEL)]
                 + [cum, cum[last_row:last_row + 1] - cum])
    decay = [jnp.exp2(e) for e in exponents]
    decay16 = [d.astype(BF16) for d in decay]
    dec16 = lambda blk, j: decay16[blk][:, j * dk:(j + 1) * dk]
    q16, key16 = q.astype(BF16), key.astype(BF16)
    qs = [chunk(q16, j) for j in range(G)]
    ks = [chunk(key16, j) for j in range(G)]
    vs = [chunk(v, j) for j in range(G)]
    qk = jnp.sum(q * key, axis=-1, keepdims=True)

    zero = jnp.zeros((L, dk), BF16)
    block_diag = lambda a, b: jnp.concatenate([jnp.concatenate([a, zero], axis=1),
                                               jnp.concatenate([zero, b], axis=1)], axis=0)
    row = lax.broadcasted_iota(jnp.int32, (L, 2 * L), 0)
    col = lax.broadcasted_iota(jnp.int32, (L, 2 * L), 1)
    pairs = range(0, G, 2)
    a = [jnp.where(col == row, chunk(qk, j), jnp.where(col == row + L, chunk(qk, j + 1), 0.0)) for j in pairs]
    for lev in range(HG_LEVELS):
        for i, j in enumerate(pairs):
            e0, e1 = dec16(lev, j), dec16(lev, j + 1)
            q_pair = jnp.concatenate([qs[j] * e0, qs[j + 1] * e1], axis=1)
            a[i] = a[i] + m_ref[lev] * _dot_nt(q_pair, block_diag(ks[j] * e0, ks[j + 1] * e1))
    o_local = []
    for i, j in enumerate(pairs):
        o_pair = _dot(a[i].astype(BF16), block_diag(vs[j], vs[j + 1]))
        o_local += [o_pair[:, :dk], o_pair[:, dk:]]
    q_in = [qs[j] * dec16(HG_LEVELS, j) for j in range(G)]
    st_add = [_dot_tn(vs[j], ks[j] * dec16(HG_LEVELS + 1, j)) for j in range(G)]
    e_all = [decay[HG_LEVELS][last_row:last_row + 1, j * dk:(j + 1) * dk] for j in range(G)]

    st = jnp.zeros((dk, dk), F32)
    for j in (reversed(range(G)) if reverse else range(G)):
        emit, c = owner[j]
        emit(pl.ds(c * L, L), o_local[j] + _dot_nt(q_in[j], st.astype(BF16)))
        st = st * e_all[j] + st_add[j]


def _hgrn_kernel(q_ref, v_ref, zf_ref, zb_ref, g_ref, qc_ref, vc_ref, zfc_ref, zbc_ref, gc_ref,
                 lb_ref, ng_ref, pf_ref, pb_ref, mf_ref, mb_ref, *rest, ctx_out):
    if ctx_out:
        o_ref, oc_ref, of_ref, ofc_ref = rest
    else:
        o_ref, of_ref = rest
        oc_ref = ofc_ref = None
    norm_g = ng_ref[...]

    def readout(o, g):
        o = o * lax.rsqrt(jnp.mean(o * o, axis=-1, keepdims=True) + RMS_EPS) * norm_g
        return (o * _silu(g)).astype(BF16)

    def store_to(dst_ref):
        def emit(rows, o):
            dst_ref[rows, :] = o
        return emit

    def readout_to(dst_ref, fwd_ref, gate_ref):
        def emit(rows, o):
            dst_ref[rows, :] = readout(fwd_ref[rows, :] + o, gate_ref[rows, :])
        return emit

    def drop(rows, o):
        pass

    lbf = (lb_ref[0, 0], lb_ref[0, 1], lb_ref[0, 2])
    _hgrn_direction([(qc_ref, vc_ref, zfc_ref, store_to(ofc_ref) if ctx_out else drop),
                     (q_ref, v_ref, zf_ref, store_to(of_ref))], lbf, pf_ref, mf_ref, False)

    lbb = (lb_ref[1, 0], lb_ref[1, 1], lb_ref[1, 2])
    _hgrn_direction([(q_ref, v_ref, zb_ref, readout_to(o_ref, of_ref, g_ref)),
                     (qc_ref, vc_ref, zbc_ref, readout_to(oc_ref, ofc_ref, gc_ref) if ctx_out else drop)],
                    lbb, pb_ref, mb_ref, True)


def _hgrn2(main, ctx, lb, norm_g, ctx_out):
    q = main[0]
    bsz, t, _ = q.shape
    lc = ctx[0].shape[1]
    pf, pb, mf, mb = _hgrn_tables()
    lb = lb.astype(F32)
    lb_params = jnp.stack([jnp.log(lb), jnp.log1p(-lb), 1.0 - lb], axis=1)
    lb_params = jnp.transpose(lb_params, (2, 0, 1, 3))[:, :, :, None, :]
    seq_spec = pl.BlockSpec((None, t, HG_HEAD_DIM), lambda b, h: (b, 0, h))
    ctx_spec = pl.BlockSpec((None, lc, HG_HEAD_DIM), lambda b, h: (b, 0, h))
    in_specs = ([seq_spec] * 5 + [ctx_spec] * 5 + [
        pl.BlockSpec((None, 2, 3, 1, HG_HEAD_DIM), lambda b, h: (h, 0, 0, 0, 0)),
        pl.BlockSpec((None, 1, HG_HEAD_DIM), lambda b, h: (h, 0, 0)),
        _const_spec(pf.shape), _const_spec(pb.shape), _const_spec(mf.shape), _const_spec(mb.shape)])
    out_specs = [seq_spec]
    out_shape = [jax.ShapeDtypeStruct((bsz, t, HG_DIM), BF16)]
    assert t % (2 * HG_CHUNK) == 0 and lc % (2 * HG_CHUNK) == 0
    scratch = [pltpu.VMEM((t, HG_HEAD_DIM), F32)]
    if ctx_out:
        out_specs.append(ctx_spec)
        out_shape.append(jax.ShapeDtypeStruct((bsz, lc, HG_DIM), BF16))
        scratch.append(pltpu.VMEM((lc, HG_HEAD_DIM), F32))
    outs = pl.pallas_call(
        functools.partial(_hgrn_kernel, ctx_out=ctx_out),
        grid=(bsz, HG_HEADS),
        in_specs=in_specs,
        out_specs=out_specs,
        out_shape=out_shape,
        scratch_shapes=scratch,
        compiler_params=_params(2),
        name="hgrn2",
    )(*main, *ctx, lb_params, norm_g.astype(F32)[:, None, :],
      jnp.asarray(pf, BF16), jnp.asarray(pb, BF16), jnp.asarray(mf), jnp.asarray(mb))
    return (outs[0], outs[1]) if ctx_out else (outs[0], None)


LANE_BLOCK = 128
CONV_ROW_BLOCK = 256


def _conv_kernel(prev_ref, cur_ref, next_ref, w_ref, b_ref, g_ref, beta_ref, o_ref, win_ref, acc_ref,
                 shift_ref):
    i = pl.program_id(1)
    tm = cur_ref.shape[0]
    win_ref[0:CONV_HALO, :] = jnp.where(i > 0, prev_ref[...], 0.0)
    win_ref[CONV_HALO:CONV_HALO + tm, :] = cur_ref[...]
    win_ref[CONV_HALO + tm:, :] = jnp.where(i < pl.num_programs(1) - 1, next_ref[...], 0.0)
    first = CONV_HALO - CONV_WIDTH // 2
    span = tm + 2 * CONV_HALO - SUBLANES
    def lane_block(c, carry):
        cols = pl.ds(pl.multiple_of(c * LANE_BLOCK, LANE_BLOCK), LANE_BLOCK)
        for phase in range(SUBLANES):
            shift_ref[phase] = win_ref[phase:phase + span, cols]
        for r0 in range(0, tm, CONV_ROW_BLOCK):
            acc = jnp.zeros((CONV_ROW_BLOCK, LANE_BLOCK), F32)
            for k in range(CONV_WIDTH):
                phase = (first + k) % SUBLANES
                off = first + k - phase + r0
                acc = acc + shift_ref[phase, off:off + CONV_ROW_BLOCK, :] * w_ref[k, :, cols]
            acc_ref[r0:r0 + CONV_ROW_BLOCK, cols] = acc + b_ref[:, cols]
        return carry

    lax.fori_loop(0, D_MODEL // LANE_BLOCK, lane_block, 0)
    y = _layer_norm(acc_ref[...], g_ref[...], beta_ref[...])
    o_ref[...] = _silu(y).astype(o_ref.dtype)


def _conv_module_mid(u, w_dw, b_dw, ln_g, ln_b, tm):
    bsz, t, d = u.shape
    hb = tm // CONV_HALO
    n_hb = t // CONV_HALO
    vec = lambda a: a.astype(F32).reshape(1, d)
    w_taps = w_dw.astype(F32).reshape(CONV_WIDTH, 1, d)
    return pl.pallas_call(
        _conv_kernel,
        grid=(bsz, t // tm),
        in_specs=[
            pl.BlockSpec((None, CONV_HALO, d), lambda b, i: (b, jnp.maximum(i * hb - 1, 0), 0)),
            pl.BlockSpec((None, tm, d), lambda b, i: (b, i, 0)),
            pl.BlockSpec((None, CONV_HALO, d), lambda b, i: (b, jnp.minimum((i + 1) * hb, n_hb - 1), 0)),
            _const_spec((CONV_WIDTH, 1, d)), _const_spec((1, d)), _const_spec((1, d)), _const_spec((1, d)),
        ],
        out_specs=pl.BlockSpec((None, tm, d), lambda b, i: (b, i, 0)),
        out_shape=jax.ShapeDtypeStruct((bsz, t, d), BF16),
        scratch_shapes=[pltpu.VMEM((tm + 2 * CONV_HALO, d), F32), pltpu.VMEM((tm, d), F32),
                        pltpu.VMEM((SUBLANES, tm + 2 * CONV_HALO - SUBLANES, LANE_BLOCK), F32)],
        compiler_params=_params(2),
        name="conv_module",
    )(u, u, u, w_taps, vec(b_dw), vec(ln_g), vec(ln_b))


def _ffn_kernel(*refs, n_act):
    x_ref = refs[0]
    act_refs = refs[1:1 + n_act]
    wo_refs = refs[1 + n_act:1 + 2 * n_act]
    (bo_ref, g1_ref, sh2_ref, sc2_ref, g2_ref, l1g_ref, l1b_ref, l2g_ref, l2b_ref,
     w1_ref, w3_ref, w2_ref, o_ref) = refs[1 + 2 * n_act:]
    sub = x_ref.shape[0] // FFN_ROW_SPLIT
    rows = [slice(r * sub, (r + 1) * sub) for r in range(FFN_ROW_SPLIT)]

    def out_proj(r):
        y = bo_ref[...]
        for a_ref, w_ref in zip(act_refs, wo_refs):
            y = y + _dot(a_ref[rows[r], :], w_ref[...])
        return y

    def norm1(r, y):
        x1 = _layer_norm(ALPHA * x_ref[rows[r], :] + g1_ref[...] * y, l1g_ref[...], l1b_ref[...])
        return x1, (x1 * (1.0 + sc2_ref[...]) + sh2_ref[...]).astype(BF16)

    def up(r, h):
        return _dot(h, w1_ref[...]), _dot(h, w3_ref[...])

    def gate(r, h1, h3):
        return (_silu(h1) * h3).astype(BF16)

    def down(r, u):
        return _dot(u, w2_ref[...])

    def norm2(r, x1, f):
        o_ref[rows[r], :] = _layer_norm(ALPHA * x1 + g2_ref[...] * f, l2g_ref[...], l2b_ref[...])

    stages = 6
    state = [None] * FFN_ROW_SPLIT
    for step in range(stages + FFN_ROW_SPLIT - 1):
        for r in range(FFN_ROW_SPLIT):
            stage = step - r
            if stage == 0:
                state[r] = out_proj(r)
            elif stage == 1:
                state[r] = norm1(r, state[r])
            elif stage == 2:
                x1, h = state[r]
                state[r] = (x1, up(r, h))
            elif stage == 3:
                x1, (h1, h3) = state[r]
                state[r] = (x1, gate(r, h1, h3))
            elif stage == 4:
                x1, u = state[r]
                state[r] = (x1, down(r, u))
            elif stage == 5:
                norm2(r, *state[r])


def _out_ffn(x, acts, w_outs, b_out, mods, ln, w1, w3, w2, tm):
    bsz, t, d = x.shape
    n_act = len(acts)
    tok = lambda width: pl.BlockSpec((None, tm, width), lambda b, i: (b, i, 0))
    vec = lambda a: a.astype(F32).reshape(1, d)
    g1, sh2, sc2, g2 = mods
    in_specs = ([tok(d)] + [tok(a.shape[-1]) for a in acts] + [_const_spec(w.shape) for w in w_outs]
                + [_const_spec((1, d))] + [_mod_spec(m) for m in mods] + [_const_spec((1, d))] * 4
                + [_const_spec(w1.shape), _const_spec(w3.shape), _const_spec(w2.shape)])
    return pl.pallas_call(
        functools.partial(_ffn_kernel, n_act=n_act),
        grid=(bsz, t // tm),
        in_specs=in_specs,
        out_specs=tok(d),
        out_shape=jax.ShapeDtypeStruct((bsz, t, d), F32),
        compiler_params=_params(2),
        name="out_ffn",
    )(x, *acts, *w_outs, vec(b_out), g1, sh2, sc2, g2, *[vec(a) for a in ln], w1, w3, w2)


def kernel(x, c, ctx, c_ctx, w_mod, b_mod, ln1_g, ln1_b, ln2_g, ln2_b, w_in_even, w_out_even, na_rpb, hg_lb,
           hg_norm_g, cv_w1, cv_b1, cv_wdw, cv_bdw, cv_ln_g, cv_ln_b, cv_w2, cv_b2, ffn_w1, ffn_w3, ffn_w2):
    bsz, t, d = x.shape
    lc = ctx.shape[1]
    depth = w_mod.shape[0]
    last_even = depth - 1 if (depth - 1) % 2 == 0 else depth - 2
    tm = min(512, t)
    tmc = min(512, lc)

    p_lb = jax.nn.softmax(hg_lb.astype(F32), axis=1)
    hg_lower = jnp.maximum(jnp.cumsum(p_lb, axis=1) - p_lb[:, :1], 0.0)

    c_all = jnp.zeros((MOD_ROWS, d), F32).at[:bsz].set(c).at[bsz].set(c_ctx)
    mod_all = _modulation(c_all, w_mod, b_mod)

    xc = ctx
    for l in range(depth):
        ctx_used = l <= last_even
        ctx_out = l < last_even
        mod = [mod_all[l, :bsz, k * d:(k + 1) * d].reshape(bsz, 1, d) for k in range(6)]
        mod_c = [mod_all[l, bsz:bsz + 1, k * d:(k + 1) * d].reshape(1, 1, d) for k in range(6)]
        ln = (ln1_g[l], ln1_b[l], ln2_g[l], ln2_b[l])
        w1, w3, w2 = ffn_w1[l].astype(BF16), ffn_w3[l].astype(BF16), ffn_w2[l].astype(BF16)
        j = l // 2
        if l % 2 == 0:
            w_in = w_in_even[j].astype(BF16)
            w_out = w_out_even[j].astype(BF16)
            w_outs = [w_out[:NA_DIM], w_out[NA_DIM:]]
            b_out = jnp.zeros((d,), F32)
            qa, ka, va, qb, ib, zf, zb, gb = _inproj_even(x, mod[0], mod[1], w_in, tm)
            qa_c, ka_c, va_c, qb_c, ib_c, zf_c, zb_c, gb_c = _inproj_even(xc, mod_c[0], mod_c[1], w_in, tmc)
            o_na = _neighbourhood_attention(qa, ka, va, ka_c, va_c, na_rpb[j])
            o_hg, o_hg_c = _hgrn2((qb, ib, zf, zb, gb), (qb_c, ib_c, zf_c, zb_c, gb_c), hg_lower[:, j],
                                  hg_norm_g[j], ctx_out)
            acts = [o_na, o_hg]
            if ctx_out:
                acts_c = [_context_attention(qa_c, ka_c, va_c), o_hg_c]
        else:
            w_outs = [cv_w2[j].astype(BF16)]
            b_out = cv_b2[j]
            conv = (cv_wdw[j], cv_bdw[j], cv_ln_g[j], cv_ln_b[j])
            w_cv1 = cv_w1[j].astype(BF16)
            b_cv1 = cv_b1[j].astype(F32).reshape(1, 2 * d)
            acts = [_conv_module_mid(_inproj_odd(x, mod[0], mod[1], w_cv1, b_cv1, tm), *conv, tm)]
            if ctx_out:
                acts_c = [_conv_module_mid(_inproj_odd(xc, mod_c[0], mod_c[1], w_cv1, b_cv1, tmc), *conv, tmc)]
        x = _out_ffn(x, acts, w_outs, b_out, mod[2:], ln, w1, w3, w2, tm)
        if ctx_out:
            xc = _out_ffn(xc, acts_c, w_outs, b_out, mod_c[2:], ln, w1, w3, w2, tmc)
    return x
```

```python
import functools

import numpy as np
import jax
import jax.numpy as jnp
from jax import lax
from jax.experimental import pallas as pl
from jax.experimental.pallas import tpu as pltpu

F32 = jnp.float32
BF16 = jnp.bfloat16

D_MODEL = 1024
DEPTH = 4
GRID_W = 64
NA_HEADS = 8
NA_HEAD_DIM = 64
NA_DIM = NA_HEADS * NA_HEAD_DIM
NA_WIN_H = 8
NA_WIN_W = 16
HG_HEADS = 4
HG_HEAD_DIM = 128
HG_DIM = HG_HEADS * HG_HEAD_DIM
HG_CHUNK = 64
HG_LEVELS = 6
SUBLANES = 8
HG_MATMUL_LEVEL = 4
LOG2_E = 1.4426950408889634
CONV_WIDTH = 31
CONV_HALO = 16
D_FF = 2816
ALPHA = (2 * DEPTH) ** 0.25
LN_EPS = 1e-5
FFN_ROW_SPLIT = 2
RMS_EPS = 1e-6

VMEM_LIMIT_BYTES = 56 * 1024 * 1024
MOD_ROWS = 24


def _params(n_grid, vmem=VMEM_LIMIT_BYTES):
    return pltpu.CompilerParams(dimension_semantics=("arbitrary",) * n_grid, vmem_limit_bytes=vmem)


def _const_spec(shape):
    nd = len(shape)
    return pl.BlockSpec(shape, lambda *_: (0,) * nd, pipeline_mode=pl.Buffered(1))


def _layer_norm(x, g, b):
    mu = jnp.mean(x, axis=-1, keepdims=True)
    xc = x - mu
    var = jnp.mean(xc * xc, axis=-1, keepdims=True)
    return xc * lax.rsqrt(var + LN_EPS) * g + b


def _silu(x):
    return x * jax.nn.sigmoid(x)


def _dot(a, b):
    return jnp.dot(a, b, preferred_element_type=F32)


def _dot_nt(a, b):
    return lax.dot_general(a, b, (((1,), (1,)), ((), ())), preferred_element_type=F32)


def _dot_tn(a, b):
    return lax.dot_general(a, b, (((0,), (0,)), ((), ())), preferred_element_type=F32)


def _mod_kernel(c_ref, w_ref, b_ref, o_ref):
    s = _silu(c_ref[...]).astype(BF16)
    o_ref[...] = _dot(s, w_ref[...].astype(BF16)) + b_ref[...]


def _modulation(c_all, w_mod, b_mod):
    depth, d, n = w_mod.shape
    tn = 1536
    return pl.pallas_call(
        _mod_kernel,
        grid=(depth, n // tn),
        in_specs=[
            pl.BlockSpec((MOD_ROWS, d), lambda l, j: (0, 0)),
            pl.BlockSpec((None, d, tn), lambda l, j: (l, 0, j)),
            pl.BlockSpec((None, 1, tn), lambda l, j: (l, 0, j)),
        ],
        out_specs=pl.BlockSpec((None, MOD_ROWS, tn), lambda l, j: (l, 0, j)),
        out_shape=jax.ShapeDtypeStruct((depth, MOD_ROWS, n), F32),
        compiler_params=_params(2),
        name="modulation",
    )(c_all, w_mod, b_mod.reshape(depth, 1, n))


def _mod_spec(mod):
    per_batch = 1 if mod.shape[0] > 1 else 0
    return pl.BlockSpec((None, 1, D_MODEL), lambda b, i: (b * per_batch, 0, 0))


def _inproj_even_kernel(x_ref, sh_ref, sc_ref, w_ref, qa_ref, ka_ref, va_ref, qb_ref, ib_ref, zf_ref,
                        zb_ref, gb_ref):
    h = (x_ref[...] * (1.0 + sc_ref[...]) + sh_ref[...]).astype(BF16)
    outs = (qa_ref, ka_ref, va_ref, qb_ref, ib_ref, zf_ref, zb_ref, gb_ref)
    for s, o_ref in enumerate(outs):
        y = _dot(h, w_ref[:, s * NA_DIM:(s + 1) * NA_DIM])
        if s == 0:
            y = y * (NA_HEAD_DIM ** -0.5 * LOG2_E)
        o_ref[...] = y.astype(o_ref.dtype)


def _inproj_even(x, sh, sc, w_bf16, tm):
    bsz, t, d = x.shape
    tok = pl.BlockSpec((None, tm, d), lambda b, i: (b, i, 0))
    out_spec = pl.BlockSpec((None, tm, NA_DIM), lambda b, i: (b, i, 0))
    dtypes = (BF16, BF16, BF16, F32, BF16, F32, F32, F32)
    return pl.pallas_call(
        _inproj_even_kernel,
        grid=(bsz, t // tm),
        in_specs=[tok, _mod_spec(sh), _mod_spec(sc), _const_spec(w_bf16.shape)],
        out_specs=[out_spec] * 8,
        out_shape=[jax.ShapeDtypeStruct((bsz, t, NA_DIM), dt) for dt in dtypes],
        compiler_params=_params(2),
        name="inproj_even",
    )(x, sh, sc, w_bf16)


def _inproj_odd_kernel(x_ref, sh_ref, sc_ref, w_ref, b_ref, u_ref):
    h = (x_ref[...] * (1.0 + sc_ref[...]) + sh_ref[...]).astype(BF16)
    a = _dot(h, w_ref[:, :D_MODEL]) + b_ref[:, :D_MODEL]
    g = _dot(h, w_ref[:, D_MODEL:]) + b_ref[:, D_MODEL:]
    u_ref[...] = a * jax.nn.sigmoid(g)


def _inproj_odd(x, sh, sc, w_bf16, bias, tm):
    bsz, t, d = x.shape
    tok = pl.BlockSpec((None, tm, d), lambda b, i: (b, i, 0))
    return pl.pallas_call(
        _inproj_odd_kernel,
        grid=(bsz, t // tm),
        in_specs=[tok, _mod_spec(sh), _mod_spec(sc), _const_spec(w_bf16.shape), _const_spec(bias.shape)],
        out_specs=tok,
        out_shape=jax.ShapeDtypeStruct((bsz, t, d), F32),
        compiler_params=_params(2),
        name="inproj_odd",
    )(x, sh, sc, w_bf16, bias)


HEADS_PER_GROUP = 4
GROUP_W = HEADS_PER_GROUP * NA_HEAD_DIM
NA_TILE_ROWS = 4
NA_TILES_PER_STEP = 2


def _softmax_pv(s_parts, v_parts):
    m = s_parts[0].max(axis=-1, keepdims=True)
    for s in s_parts[1:]:
        m = jnp.maximum(m, s.max(axis=-1, keepdims=True))
    ps = [jnp.exp2(s - m) for s in s_parts]
    l = ps[0].sum(axis=-1, keepdims=True)
    for p in ps[1:]:
        l = l + p.sum(axis=-1, keepdims=True)
    o = _dot(ps[0].astype(BF16), v_parts[0])
    for p, v in zip(ps[1:], v_parts[1:]):
        o = o + _dot(p.astype(BF16), v)
    return o * (1.0 / l)


def _na_kernel(q_ref, k_ref, v_ref, kc_ref, vc_ref, bias_ref, o_ref, *, rows, key_rows):
    n_tiles = rows // NA_TILE_ROWS
    nq = NA_TILE_ROWS * GRID_W
    nk = key_rows * GRID_W
    lane = lax.broadcasted_iota(jnp.int32, (nq, GROUP_W), 1)
    head_lanes = [(lane >= hh * NA_HEAD_DIM) & (lane < (hh + 1) * NA_HEAD_DIM) for hh in range(HEADS_PER_GROUP)]
    for t in range(NA_TILES_PER_STEP):
        tile = pl.program_id(1) * NA_TILES_PER_STEP + t
        first_row = jnp.clip(tile * NA_TILE_ROWS - NA_WIN_H // 2, 0, rows - key_rows)
        start = pl.multiple_of(first_row * GRID_W, GRID_W)
        pattern = jnp.minimum(tile, 1) + (tile == n_tiles - 1).astype(jnp.int32)
        q_rows = slice(t * nq, (t + 1) * nq)
        for g in range(NA_HEADS // HEADS_PER_GROUP):
            cols = slice(g * GROUP_W, (g + 1) * GROUP_W)
            qg = q_ref[q_rows, cols]
            kg = k_ref[pl.ds(start, nk), cols]
            vg = v_ref[pl.ds(start, nk), cols]
            kcg = kc_ref[:, cols]
            vcg = vc_ref[:, cols]
            scores = []
            for hh in range(HEADS_PER_GROUP):
                qh = jnp.where(head_lanes[hh], qg, jnp.zeros_like(qg))
                scores.append([_dot_nt(qh, kg) + bias_ref[pattern, g * HEADS_PER_GROUP + hh], _dot_nt(qh, kcg)])
            acc = jnp.zeros((nq, GROUP_W), F32)
            for hh in range(HEADS_PER_GROUP):
                acc = jnp.where(head_lanes[hh], _softmax_pv(scores[hh], [vg, vcg]), acc)
            o_ref[q_rows, cols] = acc.astype(o_ref.dtype)


def _na_tile_geometry(rows):
    kr = min(NA_WIN_H, rows)
    key_rows = min(NA_TILE_ROWS + NA_WIN_H, rows)
    outside = 2 * NA_WIN_H - 1
    geo = []
    for r0 in range(0, rows, NA_TILE_ROWS):
        first = int(np.clip(r0 - NA_WIN_H // 2, 0, rows - key_rows))
        idx = np.full((NA_TILE_ROWS, key_rows), outside, np.int32)
        for i in range(NA_TILE_ROWS):
            r = r0 + i
            rs = int(np.clip(r - kr // 2, 0, rows - kr))
            assert first <= rs and rs + kr <= first + key_rows
            for j in range(rs - first, rs - first + kr):
                idx[i, j] = first + j - r + NA_WIN_H - 1
        geo.append(idx)
    return key_rows, geo


def _na_bias_table(rpb, rows):
    key_rows, geo = _na_tile_geometry(rows)
    assert all(np.array_equal(g, geo[1]) for g in geo[1:-1])
    col = np.arange(GRID_W)
    col_start = np.clip(col - NA_WIN_W // 2, 0, GRID_W - NA_WIN_W)
    col_in = (col[None, :] >= col_start[:, None]) & (col[None, :] < col_start[:, None] + NA_WIN_W)
    dc_idx = np.clip(col[None, :] - col[:, None] + NA_WIN_W - 1, 0, 2 * NA_WIN_W - 2)
    per_dr = jnp.where(col_in[None, None], rpb[:, :, dc_idx].astype(F32) * LOG2_E, -jnp.inf)
    per_dr = jnp.concatenate([per_dr, jnp.full_like(per_dr[:, :1], -jnp.inf)], axis=1)
    tabs = []
    for idx in (geo[0], geo[1], geo[-1]):
        tabs.append(jnp.concatenate(
            [jnp.concatenate([per_dr[:, int(idx[i, j])] for j in range(key_rows)], axis=-1)
             for i in range(NA_TILE_ROWS)], axis=1))
    return key_rows, jnp.stack(tabs)


def _neighbourhood_attention(q, k, v, kc, vc, rpb):
    bsz, t, _ = q.shape
    lc = kc.shape[1]
    rows = t // GRID_W
    n_tiles = rows // NA_TILE_ROWS
    assert n_tiles >= 3 and n_tiles % NA_TILES_PER_STEP == 0
    key_rows, bias = _na_bias_table(rpb, rows)
    nq = NA_TILES_PER_STEP * NA_TILE_ROWS * GRID_W
    tile_spec = pl.BlockSpec((None, nq, NA_DIM), lambda b, i: (b, i, 0))
    seq_spec = pl.BlockSpec((None, t, NA_DIM), lambda b, i: (b, 0, 0))
    ctx_spec = pl.BlockSpec((None, lc, NA_DIM), lambda b, i: (b, 0, 0))
    return pl.pallas_call(
        functools.partial(_na_kernel, rows=rows, key_rows=key_rows),
        grid=(bsz, n_tiles // NA_TILES_PER_STEP),
        in_specs=[tile_spec, seq_spec, seq_spec, ctx_spec, ctx_spec, _const_spec(bias.shape)],
        out_specs=tile_spec,
        out_shape=jax.ShapeDtypeStruct((bsz, t, NA_DIM), BF16),
        compiler_params=_params(2),
        name="neighbourhood_attention",
    )(q, k, v, kc, vc, bias)


def _ctx_attn_kernel(q_ref, k_ref, v_ref, o_ref):
    nq = q_ref.shape[0]
    lane = lax.broadcasted_iota(jnp.int32, (nq, GROUP_W), 1)
    for g in range(NA_HEADS // HEADS_PER_GROUP):
        cols = slice(g * GROUP_W, (g + 1) * GROUP_W)
        qg = q_ref[:, cols]
        kg = k_ref[:, cols]
        vg = v_ref[:, cols]
        acc = jnp.zeros((nq, GROUP_W), F32)
        for hh in range(HEADS_PER_GROUP):
            head_lanes = (lane >= hh * NA_HEAD_DIM) & (lane < (hh + 1) * NA_HEAD_DIM)
            qh = jnp.where(head_lanes, qg, jnp.zeros_like(qg))
            o = _softmax_pv([_dot_nt(qh, kg)], [vg])
            acc = jnp.where(head_lanes, o, acc)
        o_ref[:, cols] = acc.astype(o_ref.dtype)


def _context_attention(q, k, v):
    bsz, lc, _ = q.shape
    spec = pl.BlockSpec((None, lc, NA_DIM), lambda b: (b, 0, 0))
    return pl.pallas_call(
        _ctx_attn_kernel,
        grid=(bsz,),
        in_specs=[spec, spec, spec],
        out_specs=spec,
        out_shape=jax.ShapeDtypeStruct((bsz, lc, NA_DIM), BF16),
        compiler_params=_params(1),
        name="context_attention",
    )(q, k, v)


def _hgrn_tables():
    L = HG_CHUNK
    t = np.arange(L)
    p_rows, masks = [], []
    for lev in range(HG_LEVELS):
        c = L >> lev
        mid = (t // c) * c + c // 2
        P = np.zeros((L, L), np.float32)
        for i in range(L):
            m = mid[i]
            if i >= m:
                P[i, m + 1:i + 1] = 1.0
            else:
                P[i, i + 1:m + 1] = 1.0
        p_rows.append(P)
        same = (t[:, None] // c) == (t[None, :] // c)
        masks.append((same & (t[:, None] >= mid[:, None]) & (t[None, :] < mid[None, :])).astype(np.float32))
    incl = np.tril(np.ones((L, L), np.float32))
    fine = p_rows[HG_MATMUL_LEVEL:]
    pf = np.concatenate([incl] + fine, axis=0)
    flip = lambda a: a[..., ::-1, ::-1]
    pb = np.concatenate([flip(incl)] + [flip(p) for p in fine], axis=0)
    side_by_side = lambda ms: np.stack([np.concatenate([m, m], axis=1) for m in ms])
    mf = side_by_side(masks)
    mb = side_by_side([flip(m) for m in masks])
    return pf, pb, mf, mb


def _hgrn_gates(z, log_lb, log1m_lb, one_m_lb):
    e = jnp.exp(-jnp.abs(z))
    inv = 1.0 / (1.0 + e)
    key = one_m_lb * jnp.where(z >= 0, e * inv, inv)
    c = log1m_lb + (jnp.minimum(z, 0.0) - jnp.log(1.0 + e))
    log_f = jnp.maximum(log_lb, c) + jnp.log(1.0 + jnp.exp(-jnp.abs(log_lb - c)))
    return key, log_f


def _hgrn_direction(parts, lbp, p_ref, m_ref, reverse):
    L, dk = HG_CHUNK, HG_HEAD_DIM
    owner = [(part[3], c) for part in parts for c in range(part[0].shape[0] // L)]
    G = len(owner)
    chunk = lambda a, j: a[j * L:(j + 1) * L]
    wide = lambda a: jnp.concatenate([chunk(a, j) for j in range(G)], axis=1)

    q, v, z = (jnp.concatenate([part[i][...] for part in parts], axis=0) for i in range(3))
    key, log_f = _hgrn_gates(z, lbp[0], lbp[1], lbp[2])
    log2_f = log_f * LOG2_E
    g_hi = log2_f.astype(BF16)
    g_lo = (log2_f - g_hi.astype(F32)).astype(BF16)
    p = p_ref[...]
    x = _dot(p, wide(g_hi)) + _dot(p, wide(g_lo))
    cum = x[:L]

    def level_exponent(c):
        pieces = []
        for i in range(L // c):
            mid = i * c + c // 2 - (1 if reverse else 0)
            ref = cum[mid:mid + 1]
            blk = cum[i * c:(i + 1) * c]
            if c >= 2 * SUBLANES:
                first_half, second_half = blk[:c // 2], blk[c // 2:]
                pieces += ([first_half - ref, ref - second_half] if reverse
                           else [ref - first_half, second_half - ref])
            else:
                pieces.append(-jnp.abs(blk - ref))
        return jnp.concatenate(pieces, axis=0)

    last_row = 0 if reverse else L - 1
    exponents = ([level_exponent(L >> lev) for lev in range(HG_MATMUL_LEVEL)]
                 + [x[(1 + i) * L:(2 + i) * L] for i in range(HG_LEVELS - HG_MATMUL_LEVEL)]
                 + [cum, cum[last_row:last_row + 1] - cum])
    decay = [jnp.exp2(e) for e in exponents]
    decay16 = [d.astype(BF16) for d in decay]
    dec16 = lambda blk, j: decay16[blk][:, j * dk:(j + 1) * dk]
    q16, key16 = q.astype(BF16), key.astype(BF16)
    qs = [chunk(q16, j) for j in range(G)]
    ks = [chunk(key16, j) for j in range(G)]
    vs = [chunk(v, j) for j in range(G)]
    qk = jnp.sum(q * key, axis=-1, keepdims=True)

    zero = jnp.zeros((L, dk), BF16)
    block_diag = lambda a, b: jnp.concatenate([jnp.concatenate([a, zero], axis=1),
                                               jnp.concatenate([zero, b], axis=1)], axis=0)
    row = lax.broadcasted_iota(jnp.int32, (L, 2 * L), 0)
    col = lax.broadcasted_iota(jnp.int32, (L, 2 * L), 1)
    pairs = range(0, G, 2)
    a = [jnp.where(col == row, chunk(qk, j), jnp.where(col == row + L, chunk(qk, j + 1), 0.0)) for j in pairs]
    for lev in range(HG_LEVELS):
        for i, j in enumerate(pairs):
            e0, e1 = dec16(lev, j), dec16(lev, j + 1)
            q_pair = jnp.concatenate([qs[j] * e0, qs[j + 1] * e1], axis=1)
            a[i] = a[i] + m_ref[lev] * _dot_nt(q_pair, block_diag(ks[j] * e0, ks[j + 1] * e1))
    o_local = []
    for i, j in enumerate(pairs):
        o_pair = _dot(a[i].astype(BF16), block_diag(vs[j], vs[j + 1]))
        o_local += [o_pair[:, :dk], o_pair[:, dk:]]
    q_in = [qs[j] * dec16(HG_LEVELS, j) for j in range(G)]
    st_add = [_dot_tn(vs[j], ks[j] * dec16(HG_LEVELS + 1, j)) for j in range(G)]
    e_all = [decay[HG_LEVELS][last_row:last_row + 1, j * dk:(j + 1) * dk] for j in range(G)]

    st = jnp.zeros((dk, dk), F32)
    for j in (reversed(range(G)) if reverse else range(G)):
        emit, c = owner[j]
        emit(pl.ds(c * L, L), o_local[j] + _dot_nt(q_in[j], st.astype(BF16)))
        st = st * e_all[j] + st_add[j]


def _hgrn_kernel(q_ref, v_ref, zf_ref, zb_ref, g_ref, qc_ref, vc_ref, zfc_ref, zbc_ref, gc_ref,
                 lb_ref, ng_ref, pf_ref, pb_ref, mf_ref, mb_ref, *rest, ctx_out):
    if ctx_out:
        o_ref, oc_ref, of_ref, ofc_ref = rest
    else:
        o_ref, of_ref = rest
        oc_ref = ofc_ref = None
    norm_g = ng_ref[...]

    def readout(o, g):
        o = o * lax.rsqrt(jnp.mean(o * o, axis=-1, keepdims=True) + RMS_EPS) * norm_g
        return (o * _silu(g)).astype(BF16)

    def store_to(dst_ref):
        def emit(rows, o):
            dst_ref[rows, :] = o
        return emit

    def readout_to(dst_ref, fwd_ref, gate_ref):
        def emit(rows, o):
            dst_ref[rows, :] = readout(fwd_ref[rows, :] + o, gate_ref[rows, :])
        return emit

    def drop(rows, o):
        pass

    lbf = (lb_ref[0, 0], lb_ref[0, 1], lb_ref[0, 2])
    _hgrn_direction([(qc_ref, vc_ref, zfc_ref, store_to(ofc_ref) if ctx_out else drop),
                     (q_ref, v_ref, zf_ref, store_to(of_ref))], lbf, pf_ref, mf_ref, False)

    lbb = (lb_ref[1, 0], lb_ref[1, 1], lb_ref[1, 2])
    _hgrn_direction([(q_ref, v_ref, zb_ref, readout_to(o_ref, of_ref, g_ref)),
                     (qc_ref, vc_ref, zbc_ref, readout_to(oc_ref, ofc_ref, gc_ref) if ctx_out else drop)],
                    lbb, pb_ref, mb_ref, True)


def _hgrn2(main, ctx, lb, norm_g, ctx_out):
    q = main[0]
    bsz, t, _ = q.shape
    lc = ctx[0].shape[1]
    pf, pb, mf, mb = _hgrn_tables()
    lb = lb.astype(F32)
    lb_params = jnp.stack([jnp.log(lb), jnp.log1p(-lb), 1.0 - lb], axis=1)
    lb_params = jnp.transpose(lb_params, (2, 0, 1, 3))[:, :, :, None, :]
    seq_spec = pl.BlockSpec((None, t, HG_HEAD_DIM), lambda b, h: (b, 0, h))
    ctx_spec = pl.BlockSpec((None, lc, HG_HEAD_DIM), lambda b, h: (b, 0, h))
    in_specs = ([seq_spec] * 5 + [ctx_spec] * 5 + [
        pl.BlockSpec((None, 2, 3, 1, HG_HEAD_DIM), lambda b, h: (h, 0, 0, 0, 0)),
        pl.BlockSpec((None, 1, HG_HEAD_DIM), lambda b, h: (h, 0, 0)),
        _const_spec(pf.shape), _const_spec(pb.shape), _const_spec(mf.shape), _const_spec(mb.shape)])
    out_specs = [seq_spec]
    out_shape = [jax.ShapeDtypeStruct((bsz, t, HG_DIM), BF16)]
    assert t % (2 * HG_CHUNK) == 0 and lc % (2 * HG_CHUNK) == 0
    scratch = [pltpu.VMEM((t, HG_HEAD_DIM), F32)]
    if ctx_out:
        out_specs.append(ctx_spec)
        out_shape.append(jax.ShapeDtypeStruct((bsz, lc, HG_DIM), BF16))
        scratch.append(pltpu.VMEM((lc, HG_HEAD_DIM), F32))
    outs = pl.pallas_call(
        functools.partial(_hgrn_kernel, ctx_out=ctx_out),
        grid=(bsz, HG_HEADS),
        in_specs=in_specs,
        out_specs=out_specs,
        out_shape=out_shape,
        scratch_shapes=scratch,
        compiler_params=_params(2),
        name="hgrn2",
    )(*main, *ctx, lb_params, norm_g.astype(F32)[:, None, :],
      jnp.asarray(pf, BF16), jnp.asarray(pb, BF16), jnp.asarray(mf), jnp.asarray(mb))
    return (outs[0], outs[1]) if ctx_out else (outs[0], None)


LANE_BLOCK = 128
CONV_ROW_BLOCK = 256


def _conv_kernel(prev_ref, cur_ref, next_ref, w_ref, b_ref, g_ref, beta_ref, o_ref, win_ref, acc_ref,
                 shift_ref):
    i = pl.program_id(1)
    tm = cur_ref.shape[0]
    win_ref[0:CONV_HALO, :] = jnp.where(i > 0, prev_ref[...], 0.0)
    win_ref[CONV_HALO:CONV_HALO + tm, :] = cur_ref[...]
    win_ref[CONV_HALO + tm:, :] = jnp.where(i < pl.num_programs(1) - 1, next_ref[...], 0.0)
    first = CONV_HALO - CONV_WIDTH // 2
    span = tm + 2 * CONV_HALO - SUBLANES
    def lane_block(c, carry):
        cols = pl.ds(pl.multiple_of(c * LANE_BLOCK, LANE_BLOCK), LANE_BLOCK)
        for phase in range(SUBLANES):
            shift_ref[phase] = win_ref[phase:phase + span, cols]
        for r0 in range(0, tm, CONV_ROW_BLOCK):
            acc = jnp.zeros((CONV_ROW_BLOCK, LANE_BLOCK), F32)
            for k in range(CONV_WIDTH):
                phase = (first + k) % SUBLANES
                off = first + k - phase + r0
                acc = acc + shift_ref[phase, off:off + CONV_ROW_BLOCK, :] * w_ref[k, :, cols]
            acc_ref[r0:r0 + CONV_ROW_BLOCK, cols] = acc + b_ref[:, cols]
        return carry

    lax.fori_loop(0, D_MODEL // LANE_BLOCK, lane_block, 0)
    y = _layer_norm(acc_ref[...], g_ref[...], beta_ref[...])
    o_ref[...] = _silu(y).astype(o_ref.dtype)


def _conv_module_mid(u, w_dw, b_dw, ln_g, ln_b, tm):
    bsz, t, d = u.shape
    hb = tm // CONV_HALO
    n_hb = t // CONV_HALO
    vec = lambda a: a.astype(F32).reshape(1, d)
    w_taps = w_dw.astype(F32).reshape(CONV_WIDTH, 1, d)
    return pl.pallas_call(
        _conv_kernel,
        grid=(bsz, t // tm),
        in_specs=[
            pl.BlockSpec((None, CONV_HALO, d), lambda b, i: (b, jnp.maximum(i * hb - 1, 0), 0)),
            pl.BlockSpec((None, tm, d), lambda b, i: (b, i, 0)),
            pl.BlockSpec((None, CONV_HALO, d), lambda b, i: (b, jnp.minimum((i + 1) * hb, n_hb - 1), 0)),
            _const_spec((CONV_WIDTH, 1, d)), _const_spec((1, d)), _const_spec((1, d)), _const_spec((1, d)),
        ],
        out_specs=pl.BlockSpec((None, tm, d), lambda b, i: (b, i, 0)),
        out_shape=jax.ShapeDtypeStruct((bsz, t, d), BF16),
        scratch_shapes=[pltpu.VMEM((tm + 2 * CONV_HALO, d), F32), pltpu.VMEM((tm, d), F32),
                        pltpu.VMEM((SUBLANES, tm + 2 * CONV_HALO - SUBLANES, LANE_BLOCK), F32)],
        compiler_params=_params(2),
        name="conv_module",
    )(u, u, u, w_taps, vec(b_dw), vec(ln_g), vec(ln_b))


def _ffn_kernel(*refs, n_act):
    x_ref = refs[0]
    act_refs = refs[1:1 + n_act]
    wo_refs = refs[1 + n_act:1 + 2 * n_act]
    (bo_ref, g1_ref, sh2_ref, sc2_ref, g2_ref, l1g_ref, l1b_ref, l2g_ref, l2b_ref,
     w1_ref, w3_ref, w2_ref, o_ref) = refs[1 + 2 * n_act:]
    sub = x_ref.shape[0] // FFN_ROW_SPLIT
    rows = [slice(r * sub, (r + 1) * sub) for r in range(FFN_ROW_SPLIT)]

    def out_proj(r):
        y = bo_ref[...]
        for a_ref, w_ref in zip(act_refs, wo_refs):
            y = y + _dot(a_ref[rows[r], :], w_ref[...])
        return y

    def norm1(r, y):
        x1 = _layer_norm(ALPHA * x_ref[rows[r], :] + g1_ref[...] * y, l1g_ref[...], l1b_ref[...])
        return x1, (x1 * (1.0 + sc2_ref[...]) + sh2_ref[...]).astype(BF16)

    def up(r, h):
        return _dot(h, w1_ref[...]), _dot(h, w3_ref[...])

    def gate(r, h1, h3):
        return (_silu(h1) * h3).astype(BF16)

    def down(r, u):
        return _dot(u, w2_ref[...])

    def norm2(r, x1, f):
        o_ref[rows[r], :] = _layer_norm(ALPHA * x1 + g2_ref[...] * f, l2g_ref[...], l2b_ref[...])

    stages = 6
    state = [None] * FFN_ROW_SPLIT
    for step in range(stages + FFN_ROW_SPLIT - 1):
        for r in range(FFN_ROW_SPLIT):
            stage = step - r
            if stage == 0:
                state[r] = out_proj(r)
            elif stage == 1:
                state[r] = norm1(r, state[r])
            elif stage == 2:
                x1, h = state[r]
                state[r] = (x1, up(r, h))
            elif stage == 3:
                x1, (h1, h3) = state[r]
                state[r] = (x1, gate(r, h1, h3))
            elif stage == 4:
                x1, u = state[r]
                state[r] = (x1, down(r, u))
            elif stage == 5:
                norm2(r, *state[r])


def _out_ffn(x, acts, w_outs, b_out, mods, ln, w1, w3, w2, tm):
    bsz, t, d = x.shape
    n_act = len(acts)
    tok = lambda width: pl.BlockSpec((None, tm, width), lambda b, i: (b, i, 0))
    vec = lambda a: a.astype(F32).reshape(1, d)
    g1, sh2, sc2, g2 = mods
    in_specs = ([tok(d)] + [tok(a.shape[-1]) for a in acts] + [_const_spec(w.shape) for w in w_outs]
                + [_const_spec((1, d))] + [_mod_spec(m) for m in mods] + [_const_spec((1, d))] * 4
                + [_const_spec(w1.shape), _const_spec(w3.shape), _const_spec(w2.shape)])
    return pl.pallas_call(
        functools.partial(_ffn_kernel, n_act=n_act),
        grid=(bsz, t // tm),
        in_specs=in_specs,
        out_specs=tok(d),
        out_shape=jax.ShapeDtypeStruct((bsz, t, d), F32),
        compiler_params=_params(2),
        name="out_ffn",
    )(x, *acts, *w_outs, vec(b_out), g1, sh2, sc2, g2, *[vec(a) for a in ln], w1, w3, w2)


def kernel(x, c, ctx, c_ctx, w_mod, b_mod, ln1_g, ln1_b, ln2_g, ln2_b, w_in_even, w_out_even, na_rpb, hg_lb,
           hg_norm_g, cv_w1, cv_b1, cv_wdw, cv_bdw, cv_ln_g, cv_ln_b, cv_w2, cv_b2, ffn_w1, ffn_w3, ffn_w2):
    bsz, t, d = x.shape
    lc = ctx.shape[1]
    depth = w_mod.shape[0]
    last_even = depth - 1 if (depth - 1) % 2 == 0 else depth - 2
    tm = min(512, t)
    tmc = min(512, lc)

    p_lb = jax.nn.softmax(hg_lb.astype(F32), axis=1)
    hg_lower = jnp.maximum(jnp.cumsum(p_lb, axis=1) - p_lb[:, :1], 0.0)

    c_all = jnp.zeros((MOD_ROWS, d), F32).at[:bsz].set(c).at[bsz].set(c_ctx)
    mod_all = _modulation(c_all, w_mod, b_mod)

    xc = ctx
    for l in range(depth):
        ctx_used = l <= last_even
        ctx_out = l < last_even
        mod = [mod_all[l, :bsz, k * d:(k + 1) * d].reshape(bsz, 1, d) for k in range(6)]
        mod_c = [mod_all[l, bsz:bsz + 1, k * d:(k + 1) * d].reshape(1, 1, d) for k in range(6)]
        ln = (ln1_g[l], ln1_b[l], ln2_g[l], ln2_b[l])
        w1, w3, w2 = ffn_w1[l].astype(BF16), ffn_w3[l].astype(BF16), ffn_w2[l].astype(BF16)
        j = l // 2
        if l % 2 == 0:
            w_in = w_in_even[j].astype(BF16)
            w_out = w_out_even[j].astype(BF16)
            w_outs = [w_out[:NA_DIM], w_out[NA_DIM:]]
            b_out = jnp.zeros((d,), F32)
            qa, ka, va, qb, ib, zf, zb, gb = _inproj_even(x, mod[0], mod[1], w_in, tm)
            qa_c, ka_c, va_c, qb_c, ib_c, zf_c, zb_c, gb_c = _inproj_even(xc, mod_c[0], mod_c[1], w_in, tmc)
            o_na = _neighbourhood_attention(qa, ka, va, ka_c, va_c, na_rpb[j])
            o_hg, o_hg_c = _hgrn2((qb, ib, zf, zb, gb), (qb_c, ib_c, zf_c, zb_c, gb_c), hg_lower[:, j],
                                  hg_norm_g[j], ctx_out)
            acts = [o_na, o_hg]
            if ctx_out:
                acts_c = [_context_attention(qa_c, ka_c, va_c), o_hg_c]
        else:
            w_outs = [cv_w2[j].astype(BF16)]
            b_out = cv_b2[j]
            conv = (cv_wdw[j], cv_bdw[j], cv_ln_g[j], cv_ln_b[j])
            w_cv1 = cv_w1[j].astype(BF16)
            b_cv1 = cv_b1[j].astype(F32).reshape(1, 2 * d)
            acts = [_conv_module_mid(_inproj_odd(x, mod[0], mod[1], w_cv1, b_cv1, tm), *conv, tm)]
            if ctx_out:
                acts_c = [_conv_module_mid(_inproj_odd(xc, mod_c[0], mod_c[1], w_cv1, b_cv1, tmc), *conv, tmc)]
        x = _out_ffn(x, acts, w_outs, b_out, mod[2:], ln, w1, w3, w2, tm)
        if ctx_out:
            xc = _out_ffn(xc, acts_c, w_outs, b_out, mod_c[2:], ln, w1, w3, w2, tmc)
    return x
```
